```python
import jax, jax.numpy as jnp
from jax import lax
import numpy as np

D_MODEL = 1024
BATCH = 16
SEQ = 2048
DEPTH = 2

N_MEM = 256
EPS = 1e-6

SB_HEADS = 8
SB_HEAD_DIM = 64
SB_WIDTH = SB_HEADS * SB_HEAD_DIM
SB_BLOCK = 128
SG_GROUPS = 4
SG_GROUP_DIM = 64
SG_WIDTH = SG_GROUPS * SG_GROUP_DIM
SG_CHUNK = 128
GLA_HEADS = 4
GLA_DK = 32
GLA_DV = 64
GLA_KEY_WIDTH = GLA_HEADS * GLA_DK
GLA_WIDTH = GLA_HEADS * GLA_DV
GLA_GATE_RANK = 16
GLA_TAU = 16.0
GLA_CHUNK = 128

MIX_WIDTH = SB_WIDTH + SG_WIDTH + GLA_WIDTH
IN_SPLITS = [SB_WIDTH, SB_WIDTH, SB_WIDTH,
             SG_WIDTH, SG_WIDTH,
             GLA_KEY_WIDTH, GLA_KEY_WIDTH,
             GLA_WIDTH, GLA_WIDTH,
             GLA_GATE_RANK]
IN_WIDTH = sum(IN_SPLITS)

X_HEADS = 4
X_HEAD_DIM = D_MODEL // X_HEADS

PEER_HEADS = 8
PEER_KEYS = 128
PEER_N = PEER_KEYS * PEER_KEYS
PEER_DQ = 128
PEER_TOPK = 16
PEER_TOKENS = 128

kernel_name = 'hybrid_sb_sgu_gla_peer_block'


def rmsnorm(x, gain):
    xf = x.astype(jnp.float32)
    y = xf * lax.rsqrt(jnp.mean(xf * xf, axis=-1, keepdims=True) + EPS)
    return (y * gain.astype(jnp.float32)).astype(x.dtype)


def stick_breaking_attention(q, k, v):
    S = q.shape[2]
    scale = SB_HEAD_DIM ** -0.5
    outs = []
    for start in range(0, S, SB_BLOCK):
        end = start + SB_BLOCK
        kb, vb = k[:, :, :end], v[:, :, :end]
        z = jnp.einsum('bhtd,bhsd->bhts', q[:, :, start:end], kb).astype(jnp.float32) * scale
        t_pos = start + jnp.arange(SB_BLOCK)[:, None]
        s_pos = jnp.arange(end)[None, :]
        mask = s_pos < t_pos
        log_not = jnp.where(mask, jax.nn.log_sigmoid(-z), 0.0)
        later = lax.cumsum(log_not, axis=3, reverse=True) - log_not
        w = jnp.where(mask, jnp.exp(jax.nn.log_sigmoid(z) + later), 0.0)
        outs.append(jnp.einsum('bhts,bhsd->bhtd', w.astype(vb.dtype), vb))
    return jnp.concatenate(outs, axis=2)


def spatial_gating(u, v, v_gain, w_s, b_s):
    B, S, _ = u.shape
    u = jax.nn.gelu(u)
    v = rmsnorm(jax.nn.gelu(v), v_gain)
    v = v.reshape(B, S // SG_CHUNK, SG_CHUNK, SG_GROUPS, SG_GROUP_DIM)
    causal = jnp.tril(jnp.ones((SG_CHUNK, SG_CHUNK), dtype=bool))
    w = jnp.where(causal, w_s, jnp.zeros_like(w_s))
    mixed = jnp.einsum('gts,bcsgd->bctgd', w, v) + b_s.T[:, :, None]
    return u * mixed.reshape(B, S, SG_WIDTH)


def gated_linear_attention(q, k, v, log_a):
    B, H, S, dk = q.shape
    dv = v.shape[-1]
    n = S // GLA_CHUNK

    def chunks(t):
        t = t.astype(jnp.float32).reshape(B, H, n, GLA_CHUNK, t.shape[-1])
        return jnp.moveaxis(t, 2, 0)

    qc = chunks(q) * dk ** -0.5
    kc, vc, gc = chunks(k), chunks(v), chunks(log_a)
    causal = jnp.tril(jnp.ones((GLA_CHUNK, GLA_CHUNK), dtype=bool))[:, :, None]

    def step(state, inp):
        qi, ki, vi, gi = inp
        b = jnp.cumsum(gi, axis=2)
        o_inter = jnp.einsum('bhtk,bhkv->bhtv', qi * jnp.exp(b), state)
        diff = b[:, :, :, None, :] - b[:, :, None, :, :]
        decay = jnp.exp(jnp.where(causal, diff, -jnp.inf))
        scores = jnp.einsum('bhtk,bhsk,bhtsk->bhts', qi, ki, decay)
        o_intra = jnp.einsum('bhts,bhsv->bhtv', scores, vi)
        b_end = b[:, :, -1:, :]
        state = (jnp.exp(b_end[:, :, 0, :, None]) * state
                 + jnp.einsum('bhsk,bhsv->bhkv', ki * jnp.exp(b_end - b), vi))
        return state, o_inter + o_intra

    state0 = jnp.zeros((B, H, dk, dv), jnp.float32)
    _, o = lax.scan(step, state0, (qc, kc, vc, gc))
    o = jnp.moveaxis(o, 0, 2).reshape(B, H, S, dv)
    return o.astype(v.dtype)


def hybrid_mixer(h, w_in, sg_v_gain, sg_w_spatial, sg_b_spatial,
                 gla_w_gate, gla_b_gate, gla_out_gain, w_out):
    B, S, _ = h.shape
    proj = h @ w_in
    offsets = np.cumsum(IN_SPLITS)[:-1].tolist()
    sb_q, sb_k, sb_v, sg_u, sg_v, g_q, g_k, g_v, g_o, g_a = jnp.split(proj, offsets, axis=-1)

    def heads(t, n_heads):
        return t.reshape(B, S, n_heads, -1).transpose(0, 2, 1, 3)

    y_sb = stick_breaking_attention(heads(sb_q, SB_HEADS), heads(sb_k, SB_HEADS), heads(sb_v, SB_HEADS))
    y_sb = y_sb.transpose(0, 2, 1, 3).reshape(B, S, SB_WIDTH)

    y_sg = spatial_gating(sg_u, sg_v, sg_v_gain, sg_w_spatial, sg_b_spatial)

    log_a = jax.nn.log_sigmoid((g_a @ gla_w_gate + gla_b_gate).astype(jnp.float32)) / GLA_TAU
    y_gla = gated_linear_attention(heads(g_q, GLA_HEADS), heads(g_k, GLA_HEADS),
                                   heads(g_v, GLA_HEADS), heads(log_a, GLA_HEADS))
    y_gla = y_gla.transpose(0, 2, 1, 3)
    y_gla = rmsnorm(y_gla, gla_out_gain.reshape(GLA_HEADS, GLA_DV)).reshape(B, S, GLA_WIDTH)
    y_gla = y_gla * jax.nn.silu(g_o)

    return jnp.concatenate([y_sb, y_sg, y_gla], axis=-1) @ w_out


def memory_cross_attention(h, mem, mem_gain, w_q, w_kv, w_o):
    B, S, D = h.shape
    m = rmsnorm(mem, mem_gain)
    q = (h @ w_q).reshape(B, S, X_HEADS, X_HEAD_DIM)
    k, v = jnp.split(m @ w_kv, 2, axis=-1)
    k = k.reshape(B, -1, X_HEADS, X_HEAD_DIM)
    v = v.reshape(B, -1, X_HEADS, X_HEAD_DIM)
    s = jnp.einsum('bthd,bmhd->bhtm', q, k).astype(jnp.float32) * X_HEAD_DIM ** -0.5
    p = jax.nn.softmax(s, axis=-1).astype(v.dtype)
    o = jnp.einsum('bhtm,bmhd->bthd', p, v).reshape(B, S, D)
    return o @ w_o


def peer_ffn(h, w_q, sub_keys, expert_u, expert_v):
    B, S, D = h.shape
    tokens = h.reshape(-1, PEER_TOKENS, D)

    def block(xt):
        T = xt.shape[0]
        q = (xt @ w_q).reshape(T, PEER_HEADS, 2, PEER_DQ // 2)
        scores = jnp.einsum('thpd,hpnd->thpn', q, sub_keys).astype(jnp.float32)
        s_top, i_top = lax.top_k(scores, PEER_TOPK)
        cand = s_top[:, :, 0, :, None] + s_top[:, :, 1, None, :]
        cand_idx = i_top[:, :, 0, :, None] * PEER_KEYS + i_top[:, :, 1, None, :]
        c_s, c_i = lax.top_k(cand.reshape(T, PEER_HEADS, -1), PEER_TOPK)
        idx = jnp.take_along_axis(cand_idx.reshape(T, PEER_HEADS, -1), c_i, axis=-1)
        g = jax.nn.softmax(c_s, axis=-1).astype(xt.dtype)
        act = jax.nn.gelu(jnp.einsum('td,thkd->thk', xt, expert_u[idx]))
        return jnp.einsum('thk,thkd->td', g * act, expert_v[idx])

    return lax.map(block, tokens).reshape(B, S, D)


def setup_inputs(seed: int = 0) -> dict:
    key = jax.random.key(seed)
    ks = jax.random.split(key, 22)
    L, D = DEPTH, D_MODEL

    def nrm(k, shape, scale):
        return jax.random.normal(k, shape, jnp.float32) * scale

    def gain(k, shape):
        return 1.0 + 0.05 * jax.random.normal(k, shape, jnp.float32)

    return {
        'x': nrm(ks[0], (BATCH, SEQ, D), 1.0),
        'mem': nrm(ks[1], (BATCH, N_MEM, D), 1.0),
        'norm_mix': gain(ks[2], (L, D)),
        'w_in': nrm(ks[3], (L, D, IN_WIDTH), D ** -0.5),
        'sg_v_gain': gain(ks[4], (L, SG_WIDTH)),
        'sg_w_spatial': nrm(ks[5], (L, SG_GROUPS, SG_CHUNK, SG_CHUNK), SG_CHUNK ** -0.5),
        'sg_b_spatial': gain(ks[6], (L, SG_GROUPS, SG_CHUNK)),
        'gla_w_gate': nrm(ks[7], (L, GLA_GATE_RANK, GLA_KEY_WIDTH), GLA_GATE_RANK ** -0.5),
        'gla_b_gate': gain(ks[8], (L, GLA_KEY_WIDTH)),
        'gla_out_gain': gain(ks[9], (L, GLA_WIDTH)),
        'w_out': nrm(ks[10], (L, MIX_WIDTH, D), MIX_WIDTH ** -0.5),
        'norm_mem': gain(ks[11], (L, D)),
        'mem_gain': gain(ks[12], (L, D)),
        'w_cq': nrm(ks[13], (L, D, D), D ** -0.5),
        'w_ckv': nrm(ks[14], (L, D, 2 * D), D ** -0.5),
        'w_co': nrm(ks[15], (L, D, D), D ** -0.5),
        'norm_ffn': gain(ks[16], (L, D)),
        'peer_w_q': nrm(ks[17], (L, D, PEER_HEADS * PEER_DQ), D ** -0.5),
        'peer_sub_keys': nrm(ks[18], (L, PEER_HEADS, 2, PEER_KEYS, PEER_DQ // 2), (PEER_DQ // 2) ** -0.5),
        'peer_u': nrm(ks[19], (L, PEER_N, D), D ** -0.5),
        'peer_v': nrm(ks[20], (L, PEER_N, D), (PEER_HEADS * PEER_TOPK) ** -0.5),
        'final_gain': gain(ks[21], (D,)),
    }


def reference(x, mem, norm_mix, w_in, sg_v_gain, sg_w_spatial, sg_b_spatial,
              gla_w_gate, gla_b_gate, gla_out_gain, w_out, norm_mem, mem_gain,
              w_cq, w_ckv, w_co, norm_ffn, peer_w_q, peer_sub_keys, peer_u, peer_v,
              final_gain):
    for l in range(DEPTH):
        h = rmsnorm(x, norm_mix[l])
        x = x + hybrid_mixer(h, w_in[l], sg_v_gain[l], sg_w_spatial[l], sg_b_spatial[l],
                             gla_w_gate[l], gla_b_gate[l], gla_out_gain[l], w_out[l])
        h = rmsnorm(x, norm_mem[l])
        x = x + memory_cross_attention(h, mem, mem_gain[l], w_cq[l], w_ckv[l], w_co[l])
        h = rmsnorm(x, norm_ffn[l])
        x = x + peer_ffn(h, peer_w_q[l], peer_sub_keys[l], peer_u[l], peer_v[l])
    return rmsnorm(x, final_gain)
```

```python
import functools

import jax
import jax.numpy as jnp
from jax import lax
from jax.experimental import pallas as pl
from jax.experimental.pallas import tpu as pltpu

EPS = 1e-6

LANES = 128
VMEM_LIMIT_BYTES = 56 * 1024 * 1024

SB_HEADS = 8
SB_HEAD_DIM = 64
SB_WIDTH = SB_HEADS * SB_HEAD_DIM
SG_GROUPS = 4
SG_GROUP_DIM = 64
SG_WIDTH = SG_GROUPS * SG_GROUP_DIM
GLA_HEADS = 4
GLA_DK = 32
GLA_DV = 64
GLA_KEY_WIDTH = GLA_HEADS * GLA_DK
GLA_WIDTH = GLA_HEADS * GLA_DV
GLA_GATE_RANK = 16
GLA_TAU = 16.0
CHUNK = 128
X_HEADS = 4
PEER_HEADS = 8
PEER_KEYS = 128
PEER_TOPK = 16
PEER_HALF = 64

BF16 = jnp.bfloat16
F32 = jnp.float32


def _params(*semantics):
    return pltpu.CompilerParams(dimension_semantics=semantics,
                                vmem_limit_bytes=VMEM_LIMIT_BYTES)


def _dot(a, b):
    return jnp.dot(a, b, preferred_element_type=F32)


def _dot_nt(a, b):
    return lax.dot_general(a, b, (((1,), (1,)), ((), ())), preferred_element_type=F32)


def _split_dot(a, b16):
    hi = a.astype(BF16)
    lo = (a - hi.astype(F32)).astype(BF16)
    return _dot(hi, b16) + _dot(lo, b16)


def _rmsnorm(x, gain):
    return x * lax.rsqrt(jnp.mean(x * x, axis=-1, keepdims=True) + EPS) * gain


def _log_sigmoid(x):
    return jnp.minimum(x, 0.0) - jnp.log1p(jnp.exp(-jnp.abs(x)))


def _full(shape):
    n = len(shape)
    return pl.BlockSpec(shape, lambda *_: (0,) * n)


def _inproj_kernel(x_ref, gain_ref, wsb_ref, wsg_ref, wgl_ref, wga_ref, wgate_ref, bgate_ref,
                   sb_ref, sg_ref, gl_ref, la_ref):
    h = _rmsnorm(x_ref[...], gain_ref[...]).astype(BF16)
    sb = _dot(h, wsb_ref[...])
    sb_ref[:, :SB_WIDTH] = (sb[:, :SB_WIDTH] * SB_HEAD_DIM ** -0.5).astype(BF16)
    sb_ref[:, SB_WIDTH:] = sb[:, SB_WIDTH:].astype(BF16)
    sg_ref[...] = _dot(h, wsg_ref[...])
    gl_ref[...] = _dot(h, wgl_ref[...])
    ga = _dot(h, wga_ref[...])
    gate = _dot(ga.astype(BF16), wgate_ref[...]) + bgate_ref[...]
    la_ref[...] = _log_sigmoid(gate) * (1.0 / GLA_TAU)


def _inproj(x2, gain, wsb, wsg, wgl, wga, wgate, bgate, tile):
    n, d = x2.shape
    row = lambda w: pl.BlockSpec((tile, w), lambda i: (i, 0))
    return pl.pallas_call(
        _inproj_kernel,
        grid=(n // tile,),
        in_specs=[row(d), _full(gain.shape), _full(wsb.shape), _full(wsg.shape), _full(wgl.shape),
                  _full(wga.shape), _full(wgate.shape), _full(bgate.shape)],
        out_specs=[row(3 * SB_WIDTH), row(2 * SG_WIDTH), row(2 * GLA_KEY_WIDTH + 2 * GLA_WIDTH),
                   row(GLA_KEY_WIDTH)],
        out_shape=[jax.ShapeDtypeStruct((n, 3 * SB_WIDTH), BF16),
                   jax.ShapeDtypeStruct((n, 2 * SG_WIDTH), F32),
                   jax.ShapeDtypeStruct((n, 2 * GLA_KEY_WIDTH + 2 * GLA_WIDTH), F32),
                   jax.ShapeDtypeStruct((n, GLA_KEY_WIDTH), F32)],
        compiler_params=_params("parallel"),
        name="inproj",
    )(x2, gain, wsb, wsg, wgl, wga, wgate, bgate)


def _sb_kernel(q_ref, k_ref, v_ref, o_ref, acc_ref, carry_ref, *, tq):
    qi = pl.program_id(2)
    nsub = tq // CHUNK
    q = q_ref[0]
    lane = lax.broadcasted_iota(jnp.int32, (tq, LANES), 1)
    zero = jnp.zeros_like(q)
    q_heads = (jnp.where(lane < SB_HEAD_DIM, q, zero), jnp.where(lane >= SB_HEAD_DIM, q, zero))
    r_i = lax.broadcasted_iota(jnp.int32, (CHUNK, 2 * LANES), 0)
    c_i = lax.broadcasted_iota(jnp.int32, (CHUNK, 2 * LANES), 1)
    m_ext = jnp.where((c_i >= LANES) | (r_i > c_i), 1.0, 0.0).astype(BF16)
    t_loc = lax.broadcasted_iota(jnp.int32, (tq, LANES), 0)

    acc_ref[...] = jnp.zeros_like(acc_ref)
    carry_ref[...] = jnp.zeros_like(carry_ref)

    def block(kj, mask):
        start = pl.multiple_of(kj * CHUNK, CHUNK)
        kb = k_ref[0, pl.ds(start, CHUNK), :]
        vb = v_ref[0, pl.ds(start, CHUNK), :]
        for a in range(2):
            z = _dot_nt(q_heads[a], kb)
            log_not = _log_sigmoid(-z)
            if mask is not None:
                log_not = jnp.where(mask, log_not, 0.0)
            r = _split_dot(log_not, m_ext)
            w = jnp.exp(z + log_not + r[:, :LANES] + carry_ref[a])
            if mask is not None:
                w = jnp.where(mask, w, 0.0)
            acc_ref[a] += _dot(w.astype(BF16), vb)
            carry_ref[a] += r[:, LANES:]

    for c in reversed(range(nsub)):
        block(qi * nsub + c, (c * CHUNK + lane) < t_loc)

    def body(it, _):
        block(qi * nsub - 1 - it, None)
        return 0

    lax.fori_loop(0, qi * nsub, body, 0)
    o_ref[0] = jnp.where(lane < SB_HEAD_DIM, acc_ref[0], acc_ref[1]).astype(BF16)


def _sb_attn(sb, tq):
    b, s, _ = sb.shape
    pairs = SB_WIDTH // LANES
    return pl.pallas_call(
        functools.partial(_sb_kernel, tq=tq),
        grid=(b, pairs, s // tq),
        in_specs=[pl.BlockSpec((1, tq, LANES), lambda bi, hp, qi: (bi, qi, hp)),
                  pl.BlockSpec((1, s, LANES), lambda bi, hp, qi: (bi, 0, pairs + hp)),
                  pl.BlockSpec((1, s, LANES), lambda bi, hp, qi: (bi, 0, 2 * pairs + hp))],
        out_specs=pl.BlockSpec((1, tq, LANES), lambda bi, hp, qi: (bi, qi, hp)),
        out_shape=jax.ShapeDtypeStruct((b, s, SB_WIDTH), BF16),
        scratch_shapes=[pltpu.VMEM((2, tq, LANES), F32), pltpu.VMEM((2, tq, LANES), F32)],
        compiler_params=_params("parallel", "parallel", "parallel"),
        name="sb_attn",
    )(sb, sb, sb)


def _sgu_gla_kernel(sg_ref, gl_ref, la_ref, sgain_ref, ws_ref, bs_ref, ogain_ref, y_ref, state_ref):
    @pl.when(pl.program_id(1) == 0)
    def _():
        state_ref[...] = jnp.zeros_like(state_ref)

    row = lax.broadcasted_iota(jnp.int32, (CHUNK, CHUNK), 0)
    col = lax.broadcasted_iota(jnp.int32, (CHUNK, CHUNK), 1)
    causal = row >= col
    lane_w = lax.broadcasted_iota(jnp.int32, (CHUNK, SG_WIDTH), 1)

    u = jax.nn.gelu(sg_ref[0, :, :SG_WIDTH])
    v = _rmsnorm(jax.nn.gelu(sg_ref[0, :, SG_WIDTH:]), sgain_ref[...]).astype(BF16)
    mixed = bs_ref[...]
    for g in range(SG_GROUPS):
        wg = jnp.where(causal, ws_ref[g], 0.0).astype(BF16)
        in_group = (lane_w >= g * SG_GROUP_DIM) & (lane_w < (g + 1) * SG_GROUP_DIM)
        mixed = mixed + _dot(wg, jnp.where(in_group, v, jnp.zeros_like(v)))
    y_ref[0, :, :SG_WIDTH] = (u * mixed).astype(BF16)

    kw = GLA_KEY_WIDTH
    q = gl_ref[0, :, :kw] * GLA_DK ** -0.5
    k = gl_ref[0, :, kw:2 * kw]
    vv = gl_ref[0, :, 2 * kw:2 * kw + GLA_WIDTH]
    og = gl_ref[0, :, 2 * kw + GLA_WIDTH:]
    tri = jnp.where(causal, 1.0, 0.0).astype(BF16)
    la = la_ref[0]
    la_hi = la.astype(BF16)
    la_lo = (la - la_hi.astype(F32)).astype(BF16)
    b = _dot(tri, la_hi) + _dot(tri, la_lo)
    b_t = b.T
    k_t = k.T
    b_end_row = b[CHUNK - 1:CHUNK, :]
    b_end_col = b_t[:, CHUNK - 1:CHUNK]

    lane_v = lax.broadcasted_iota(jnp.int32, (CHUNK, GLA_WIDTH), 1)
    state = state_ref[...]
    o = _dot((q * jnp.exp(b)).astype(BF16), state.astype(BF16))
    vv16 = vv.astype(BF16)
    for h in range(GLA_HEADS):
        acc = jnp.zeros((CHUNK, CHUNK), F32)
        for kk in range(GLA_DK):
            c = h * GLA_DK + kk
            decay = jnp.exp(b[:, c:c + 1] - b_t[c:c + 1, :])
            acc = acc + (q[:, c:c + 1] * k_t[c:c + 1, :]) * decay
        scores = jnp.where(causal, acc, 0.0).astype(BF16)
        in_head = (lane_v >= h * GLA_DV) & (lane_v < (h + 1) * GLA_DV)
        o = o + _dot(scores, jnp.where(in_head, vv16, jnp.zeros_like(vv16)))

    kd_t = (k_t * jnp.exp(b_end_col - b_t)).astype(BF16)
    row_h = lax.broadcasted_iota(jnp.int32, (kw, GLA_WIDTH), 0) // GLA_DK
    col_h = lax.broadcasted_iota(jnp.int32, (kw, GLA_WIDTH), 1) // GLA_DV
    state_ref[...] = jnp.exp(b_end_col) * state + jnp.where(row_h == col_h, _dot(kd_t, vv16), 0.0)

    r2 = lax.broadcasted_iota(jnp.int32, (GLA_WIDTH, GLA_WIDTH), 0) // GLA_DV
    c2 = lax.broadcasted_iota(jnp.int32, (GLA_WIDTH, GLA_WIDTH), 1) // GLA_DV
    avg = jnp.where(r2 == c2, 1.0 / GLA_DV, 0.0).astype(BF16)
    msq = _split_dot(o * o, avg)
    y = o * lax.rsqrt(msq + EPS) * ogain_ref[...]
    y_ref[0, :, SG_WIDTH:] = (y * (og * (1.0 / (1.0 + jnp.exp(-og))))).astype(BF16)


def _sgu_gla(sg, gl, la, sgain, ws, bs, ogain):
    b, s, _ = sg.shape
    blk = lambda w: pl.BlockSpec((1, CHUNK, w), lambda bi, ci: (bi, ci, 0))
    return pl.pallas_call(
        _sgu_gla_kernel,
        grid=(b, s // CHUNK),
        in_specs=[blk(sg.shape[2]), blk(gl.shape[2]), blk(la.shape[2]),
                  _full(sgain.shape), _full(ws.shape), _full(bs.shape), _full(ogain.shape)],
        out_specs=blk(SG_WIDTH + GLA_WIDTH),
        out_shape=jax.ShapeDtypeStruct((b, s, SG_WIDTH + GLA_WIDTH), BF16),
        scratch_shapes=[pltpu.VMEM((GLA_KEY_WIDTH, GLA_WIDTH), F32)],
        compiler_params=_params("parallel", "arbitrary"),
        name="sgu_gla",
    )(sg, gl, la, sgain, ws, bs, ogain)


def _outproj_kernel(x_ref, ysb_ref, yrest_ref, w_ref, o_ref):
    o_ref[...] = (x_ref[...] + _dot(ysb_ref[...], w_ref[:SB_WIDTH, :])
                  + _dot(yrest_ref[...], w_ref[SB_WIDTH:, :]))


def _outproj(x2, ysb, yrest, w, tile):
    n, d = x2.shape
    row = lambda w_: pl.BlockSpec((tile, w_), lambda i: (i, 0))
    return pl.pallas_call(
        _outproj_kernel,
        grid=(n // tile,),
        in_specs=[row(d), row(ysb.shape[1]), row(yrest.shape[1]), _full(w.shape)],
        out_specs=row(d),
        out_shape=jax.ShapeDtypeStruct((n, d), F32),
        compiler_params=_params("parallel"),
        name="outproj",
    )(x2, ysb, yrest, w)


def _mem_kv_kernel(mem_ref, gain_ref, w_ref, kv_ref):
    m = _rmsnorm(mem_ref[0], gain_ref[...]).astype(BF16)
    kv_ref[0] = _dot(m, w_ref[...]).astype(BF16)


def _mem_kv(mem, gain, w):
    b, m, d = mem.shape
    return pl.pallas_call(
        _mem_kv_kernel,
        grid=(b,),
        in_specs=[pl.BlockSpec((1, m, d), lambda bi: (bi, 0, 0)), _full(gain.shape), _full(w.shape)],
        out_specs=pl.BlockSpec((1, m, 2 * d), lambda bi: (bi, 0, 0)),
        out_shape=jax.ShapeDtypeStruct((b, m, 2 * d), BF16),
        compiler_params=_params("parallel"),
        name="mem_kv",
    )(mem, gain, w)


def _cross_kernel(x_ref, gain_ref, wq_ref, kv_ref, wo_ref, o_ref, att_ref):
    x = x_ref[...]
    d = x.shape[1]
    hd = d // X_HEADS
    h = _rmsnorm(x, gain_ref[...]).astype(BF16)
    q = (_dot(h, wq_ref[...]) * hd ** -0.5).astype(BF16)
    for a in range(X_HEADS):
        kh = kv_ref[0, :, a * hd:(a + 1) * hd]
        vh = kv_ref[0, :, d + a * hd:d + (a + 1) * hd]
        s = _dot_nt(q[:, a * hd:(a + 1) * hd], kh)
        e = jnp.exp(s - jnp.max(s, axis=-1, keepdims=True))
        p = e / jnp.sum(e, axis=-1, keepdims=True)
        att_ref[:, a * hd:(a + 1) * hd] = _dot(p.astype(BF16), vh).astype(BF16)
    o_ref[...] = x + _dot(att_ref[...], wo_ref[...])


def _cross(x2, gain, wq, kv, wo, tile, seq):
    n, d = x2.shape
    tiles_per_seq = seq // tile
    row = pl.BlockSpec((tile, d), lambda i: (i, 0))
    return pl.pallas_call(
        _cross_kernel,
        grid=(n // tile,),
        in_specs=[row, _full(gain.shape), _full(wq.shape),
                  pl.BlockSpec((1,) + kv.shape[1:], lambda i: (i // tiles_per_seq, 0, 0)),
                  _full(wo.shape)],
        out_specs=row,
        out_shape=jax.ShapeDtypeStruct((n, d), F32),
        scratch_shapes=[pltpu.VMEM((tile, d), BF16)],
        compiler_params=_params("parallel"),
        name="cross",
    )(x2, gain, wq, kv, wo)


_CAND_PAIRS = [(r1, r2) for r1 in range(PEER_TOPK) for r2 in range(PEER_TOPK)
               if (r1 + 1) * (r2 + 1) <= PEER_TOPK]


def _top16(s):
    work = s
    rank = jnp.full(s.shape, float(PEER_KEYS - 1), F32)
    vals = []
    for r in range(PEER_TOPK):
        m = jnp.max(work, axis=0, keepdims=True)
        hit = work == m
        rank = jnp.where(hit, float(r), rank)
        work = jnp.where(hit, -jnp.inf, work)
        vals.append(m)
    return vals, rank


def _peer_select_kernel(x_ref, gain_ref, wqt_ref, keys_ref, ht_ref, rank2_ref, e2_ref, cnt_ref, e1_ref):
    h_t = _rmsnorm(x_ref[...], gain_ref[...]).T.astype(BF16)
    ht_ref[...] = h_t
    q_t = _dot(wqt_ref[...], h_t).astype(BF16)
    tokens = h_t.shape[1]
    for hd in range(PEER_HEADS):
        base = hd * 2 * PEER_HALF
        s1 = _dot(keys_ref[hd, 0], q_t[base:base + PEER_HALF, :])
        s2 = _dot(keys_ref[hd, 1], q_t[base + PEER_HALF:base + 2 * PEER_HALF, :])
        v1, _ = _top16(s1)
        v2, rank2 = _top16(s2)
        cands = [v1[r1] + v2[r2] for r1, r2 in _CAND_PAIRS]
        pad = -len(cands) % 8
        cands = jnp.concatenate(cands + [jnp.full((pad, tokens), -jnp.inf, F32)], axis=0)
        top = []
        for _ in range(PEER_TOPK):
            m = jnp.max(cands, axis=0, keepdims=True)
            cands = jnp.where(cands == m, -jnp.inf, cands)
            top.append(m)
        tau = top[-1]
        top = jnp.concatenate(top, axis=0)
        z = jnp.sum(jnp.exp(top - top[0:1, :]), axis=0, keepdims=True)
        cnt = jnp.zeros(s1.shape, F32)
        for r2 in range(PEER_TOPK):
            cnt = cnt + jnp.where(s1 + v2[r2] >= tau, 1.0, 0.0)
        rank2_ref[hd] = rank2.astype(BF16)
        e2_ref[hd] = jnp.exp(s2 - v2[0]).astype(BF16)
        cnt_ref[hd] = cnt
        e1_ref[hd] = jnp.exp(s1 - v1[0]) / z


def _peer_select(x2, gain, wqt, keys, tile):
    n, d = x2.shape
    per_head = lambda: pl.BlockSpec((PEER_HEADS, PEER_KEYS, tile), lambda i: (0, 0, i))
    sel_shape = (PEER_HEADS, PEER_KEYS, n)
    return pl.pallas_call(
        _peer_select_kernel,
        grid=(n // tile,),
        in_specs=[pl.BlockSpec((tile, d), lambda i: (i, 0)), _full(gain.shape), _full(wqt.shape),
                  _full(keys.shape)],
        out_specs=[pl.BlockSpec((d, tile), lambda i: (0, i)),
                   per_head(), per_head(), per_head(), per_head()],
        out_shape=[jax.ShapeDtypeStruct((d, n), BF16),
                   jax.ShapeDtypeStruct(sel_shape, BF16), jax.ShapeDtypeStruct(sel_shape, BF16),
                   jax.ShapeDtypeStruct(sel_shape, F32), jax.ShapeDtypeStruct(sel_shape, F32)],
        compiler_params=_params("parallel"),
        name="peer_select",
    )(x2, gain, wqt, keys)


def _peer_main_kernel(x_ref, ht_ref, rank2_ref, e2_ref, cnt_ref, e1_ref, u_ref, vt_ref, fgain_ref,
                      o_ref, acc_ref, act_ref, ga_ref, *, final_norm):
    j = pl.program_id(1)

    @pl.when(j == 0)
    def _():
        acc_ref[...] = jnp.zeros_like(acc_ref)

    act_ref[...] = _dot(u_ref[...], ht_ref[...])
    groups = u_ref.shape[0] // PEER_KEYS
    for ii in range(groups):
        i1 = j * groups + ii
        gate = jnp.zeros((PEER_KEYS, ht_ref.shape[1]), BF16)
        for hd in range(PEER_HEADS):
            cnt_row = cnt_ref[hd, pl.ds(i1, 1), :].astype(BF16)
            e1_row = e1_ref[hd, pl.ds(i1, 1), :].astype(BF16)
            w = e2_ref[hd] * e1_row
            gate = gate + jnp.where(rank2_ref[hd] < cnt_row, w, jnp.zeros_like(w))
        rows = slice(ii * PEER_KEYS, (ii + 1) * PEER_KEYS)
        ga_ref[rows, :] = jax.nn.gelu(act_ref[rows, :]).astype(BF16) * gate
    acc_ref[...] += _dot(vt_ref[...], ga_ref[...])

    @pl.when(j == pl.num_programs(1) - 1)
    def _():
        y = x_ref[...] + acc_ref[...].T
        if final_norm:
            y = _rmsnorm(y, fgain_ref[...])
        o_ref[...] = y


def _peer_main(x2, ht, rank2, e2, cnt, e1, u16, vt16, fgain, tile, etile, final_norm):
    n, d = x2.shape
    n_exp = u16.shape[0]
    per_head = lambda: pl.BlockSpec((PEER_HEADS, PEER_KEYS, tile), lambda i, j: (0, 0, i))
    return pl.pallas_call(
        functools.partial(_peer_main_kernel, final_norm=final_norm),
        grid=(n // tile, n_exp // etile),
        in_specs=[pl.BlockSpec((tile, d), lambda i, j: (i, 0)),
                  pl.BlockSpec((d, tile), lambda i, j: (0, i)),
                  per_head(), per_head(), per_head(), per_head(),
                  pl.BlockSpec((etile, d), lambda i, j: (j, 0)),
                  pl.BlockSpec((d, etile), lambda i, j: (0, j)),
                  pl.BlockSpec(fgain.shape, lambda i, j: (0, 0))],
        out_specs=pl.BlockSpec((tile, d), lambda i, j: (i, 0)),
        out_shape=jax.ShapeDtypeStruct((n, d), F32),
        scratch_shapes=[pltpu.VMEM((d, tile), F32), pltpu.VMEM((etile, tile), F32),
                        pltpu.VMEM((etile, tile), BF16)],
        compiler_params=_params("parallel", "arbitrary"),
        name="peer_main",
    )(x2, ht, rank2, e2, cnt, e1, u16, vt16, fgain)


def _tile(n, want):
    t = min(want, n)
    while n % t:
        t //= 2
    return t


def kernel(x, mem, norm_mix, w_in, sg_v_gain, sg_w_spatial, sg_b_spatial, gla_w_gate, gla_b_gate,
           gla_out_gain, w_out, norm_mem, mem_gain, w_cq, w_ckv, w_co, norm_ffn, peer_w_q,
           peer_sub_keys, peer_u, peer_v, final_gain):
    b, s, d = x.shape
    n = b * s
    depth = w_in.shape[0]
    row_tile = _tile(s, 512)
    sb_tile = _tile(s, 512)
    sel_tile = _tile(n, 256)
    main_tile = _tile(n, 512)
    etile = 1024

    sb_end = 3 * SB_WIDTH
    sg_end = sb_end + 2 * SG_WIDTH
    gl_end = sg_end + 2 * GLA_KEY_WIDTH + 2 * GLA_WIDTH
    x2 = x.reshape(n, d)
    for l in range(depth):
        w16 = w_in[l].astype(BF16)
        wga = jnp.pad(w16[:, gl_end:], ((0, 0), (0, LANES - GLA_GATE_RANK)))
        wgate = jnp.pad(gla_w_gate[l].astype(BF16), ((0, LANES - GLA_GATE_RANK), (0, 0)))
        sb, sg, gl, la = _inproj(x2, norm_mix[l][None], w16[:, :sb_end], w16[:, sb_end:sg_end],
                                 w16[:, sg_end:gl_end], wga, wgate, gla_b_gate[l][None], row_tile)
        ysb = _sb_attn(sb.reshape(b, s, -1), sb_tile)
        bias = jnp.repeat(sg_b_spatial[l].T, SG_GROUP_DIM, axis=1)
        yrest = _sgu_gla(sg.reshape(b, s, -1), gl.reshape(b, s, -1), la.reshape(b, s, -1),
                         sg_v_gain[l][None], sg_w_spatial[l], bias, gla_out_gain[l][None])
        x2 = _outproj(x2, ysb.reshape(n, -1), yrest.reshape(n, -1), w_out[l].astype(BF16), row_tile)

        kv = _mem_kv(mem, mem_gain[l][None], w_ckv[l].astype(BF16))
        x2 = _cross(x2, norm_mem[l][None], w_cq[l].astype(BF16), kv, w_co[l].astype(BF16),
                    row_tile, s)

        keys = peer_sub_keys[l].astype(BF16)
        ht, rank2, e2, cnt, e1 = _peer_select(x2, norm_ffn[l][None], peer_w_q[l].T.astype(BF16),
                                              keys, sel_tile)
        x2 = _peer_main(x2, ht, rank2, e2, cnt, e1, peer_u[l].astype(BF16),
                        peer_v[l].T.astype(BF16), final_gain[None], main_tile, etile,
                        final_norm=(l == depth - 1))
    return x2.reshape(b, s, d)
```

```python
import functools

import jax
import jax.numpy as jnp
import numpy as np
from jax import lax
from jax.experimental import pallas as pl
from jax.experimental.pallas import tpu as pltpu

EPS = 1e-6
LOG2E = 1.4426950408889634

LANES = 128
BF16_ROWS = 16
GELU_K1 = -2.0 * 0.7978845608028654 * LOG2E
GELU_K3 = GELU_K1 * 0.044715
VMEM_LIMIT_BYTES = 56 * 1024 * 1024

SB_HEADS = 8
SB_HEAD_DIM = 64
SB_WIDTH = SB_HEADS * SB_HEAD_DIM
SG_GROUPS = 4
SG_GROUP_DIM = 64
SG_WIDTH = SG_GROUPS * SG_GROUP_DIM
GLA_HEADS = 4
GLA_DK = 32
GLA_DV = 64
GLA_KEY_WIDTH = GLA_HEADS * GLA_DK
GLA_WIDTH = GLA_HEADS * GLA_DV
GLA_GATE_RANK = 16
GLA_TAU = 16.0
CHUNK = 128
X_HEADS = 4
PEER_HEADS = 8
PEER_KEYS = 128
PEER_TOPK = 16
PEER_HALF = 64
STAGE_ROWS = 256

BF16 = jnp.bfloat16
F32 = jnp.float32


def _params(*semantics, flags=None):
    return pltpu.CompilerParams(dimension_semantics=semantics,
                                vmem_limit_bytes=VMEM_LIMIT_BYTES, flags=flags)


def _dot(a, b):
    return jnp.dot(a, b, preferred_element_type=F32)


def _dot_nt(a, b):
    return lax.dot_general(a, b, (((1,), (1,)), ((), ())), preferred_element_type=F32)


def _split_dot(a, b16):
    hi = a.astype(BF16)
    lo = (a - hi.astype(F32)).astype(BF16)
    return _dot(hi, b16) + _dot(lo, b16)


def _rmsnorm(x, gain):
    return x * lax.rsqrt(jnp.mean(x * x, axis=-1, keepdims=True) + EPS) * gain


def _log_sigmoid(x):
    return jnp.minimum(x, 0.0) - jnp.log1p(jnp.exp(-jnp.abs(x)))


def _full(shape):
    n = len(shape)
    return pl.BlockSpec(shape, lambda *_: (0,) * n)


def _inproj_kernel(x_ref, gain_ref, wsb_ref, wsg_ref, wgl_ref, wga_ref, wgate_ref, bgate_ref,
                   sb_ref, sg_ref, gl_ref, la_ref):
    h = _rmsnorm(x_ref[...], gain_ref[...]).astype(BF16)
    sb = _dot(h, wsb_ref[...])
    sb_ref[:, :SB_WIDTH] = (sb[:, :SB_WIDTH] * (SB_HEAD_DIM ** -0.5 * LOG2E)).astype(BF16)
    sb_ref[:, SB_WIDTH:] = sb[:, SB_WIDTH:].astype(BF16)
    sg_ref[...] = _dot(h, wsg_ref[...])
    gl_ref[...] = _dot(h, wgl_ref[...])
    ga = _dot(h, wga_ref[...])
    gate = _dot(ga.astype(BF16), wgate_ref[...]) + bgate_ref[...]
    la_ref[...] = _log_sigmoid(gate) * (1.0 / GLA_TAU)


def _inproj(x2, gain, wsb, wsg, wgl, wga, wgate, bgate, tile):
    n, d = x2.shape
    row = lambda w: pl.BlockSpec((tile, w), lambda i: (i, 0))
    return pl.pallas_call(
        _inproj_kernel,
        grid=(n // tile,),
        in_specs=[row(d), _full(gain.shape), _full(wsb.shape), _full(wsg.shape), _full(wgl.shape),
                  _full(wga.shape), _full(wgate.shape), _full(bgate.shape)],
        out_specs=[row(3 * SB_WIDTH), row(2 * SG_WIDTH), row(2 * GLA_KEY_WIDTH + 2 * GLA_WIDTH),
                   row(GLA_KEY_WIDTH)],
        out_shape=[jax.ShapeDtypeStruct((n, 3 * SB_WIDTH), BF16),
                   jax.ShapeDtypeStruct((n, 2 * SG_WIDTH), F32),
                   jax.ShapeDtypeStruct((n, 2 * GLA_KEY_WIDTH + 2 * GLA_WIDTH), F32),
                   jax.ShapeDtypeStruct((n, GLA_KEY_WIDTH), F32)],
        compiler_params=_params("parallel"),
        name="inproj",
    )(x2, gain, wsb, wsg, wgl, wga, wgate, bgate)


def _sb_kernel(q_ref, k_ref, v_ref, o_ref, acc_ref, carry_ref, *, tq):
    qi = pl.program_id(2)
    nsub = tq // CHUNK
    lane = lax.broadcasted_iota(jnp.int32, (CHUNK, LANES), 1)
    head0 = lane < SB_HEAD_DIM
    r_i = lax.broadcasted_iota(jnp.int32, (2 * LANES, 2 * LANES), 0)
    c_i = lax.broadcasted_iota(jnp.int32, (2 * LANES, 2 * LANES), 1)
    m_incl = jnp.where((r_i >= c_i) & ((r_i >= LANES) == (c_i >= LANES)), -1.0, 0.0).astype(BF16)

    acc_ref[...] = jnp.zeros_like(acc_ref)
    carry_ref[...] = jnp.zeros_like(carry_ref)

    def blocks(kjs, row0, masked):
        rows = slice(row0, tq)
        nrow = tq - row0
        q = q_ref[0, rows, :]
        carry = carry_ref[rows, :]
        acc = acc_ref[rows, :]
        for kj in kjs:
            start = pl.multiple_of(kj * CHUNK, CHUNK)
            kb = k_ref[0, pl.ds(start, CHUNK), :]
            vb = v_ref[0, pl.ds(start, CHUNK), :]
            zero = jnp.zeros_like(kb)
            k2 = jnp.concatenate([jnp.where(head0, kb, zero), jnp.where(head0, zero, kb)], axis=0)
            v2 = jnp.concatenate([jnp.where(head0, vb, zero), jnp.where(head0, zero, vb)], axis=0)
            z = _dot_nt(q, k2)
            nl = jnp.maximum(z, 0.0) + jnp.log(1.0 + jnp.exp2(-jnp.abs(z))) * LOG2E
            if masked:
                t_loc = row0 + lax.broadcasted_iota(jnp.int32, (nrow, 2 * LANES), 0)
                s_loc = row0 + (lax.broadcasted_iota(jnp.int32, (nrow, 2 * LANES), 1) & (LANES - 1))
                mask = s_loc < t_loc
                nl = jnp.where(mask, nl, 0.0)
            incl = _split_dot(nl, m_incl)
            w = jnp.exp2(z + incl + carry)
            if masked:
                w = jnp.where(mask, w, 0.0)
            acc = acc + _dot(w.astype(BF16), v2)
            carry = carry + jnp.concatenate(
                [jnp.broadcast_to(incl[:, 0:1], (nrow, LANES)),
                 jnp.broadcast_to(incl[:, LANES:LANES + 1], (nrow, LANES))], axis=1)
        carry_ref[rows, :] = carry
        acc_ref[rows, :] = acc

    for c in reversed(range(nsub)):
        blocks([qi * nsub + c], c * CHUNK, True)

    def body(it, _):
        first = (qi - it) * nsub - 1
        blocks([first - u for u in range(nsub)], 0, False)
        return 0

    lax.fori_loop(0, qi, body, 0)
    o_ref[0] = acc_ref[...].astype(BF16)


def _sb_attn(sb, tq):
    b, s, _ = sb.shape
    pairs = SB_WIDTH // LANES
    return pl.pallas_call(
        functools.partial(_sb_kernel, tq=tq),
        grid=(b, pairs, s // tq),
        in_specs=[pl.BlockSpec((1, tq, LANES), lambda bi, hp, qi: (bi, qi, hp)),
                  pl.BlockSpec((1, s, LANES), lambda bi, hp, qi: (bi, 0, pairs + hp)),
                  pl.BlockSpec((1, s, LANES), lambda bi, hp, qi: (bi, 0, 2 * pairs + hp))],
        out_specs=pl.BlockSpec((1, tq, LANES), lambda bi, hp, qi: (bi, qi, hp)),
        out_shape=jax.ShapeDtypeStruct((b, s, SB_WIDTH), BF16),
        scratch_shapes=[pltpu.VMEM((tq, LANES), F32), pltpu.VMEM((tq, 2 * LANES), F32)],
        compiler_params=_params("parallel", "parallel", "parallel"),
        name="sb_attn",
    )(sb, sb, sb)


GLA_LEVELS = 7


def _gla_level_tables():
    r = np.arange(CHUNK)
    sel = np.zeros((GLA_LEVELS, CHUNK, CHUNK), np.float32)
    upper = np.zeros((GLA_LEVELS, CHUNK, GLA_KEY_WIDTH), np.float32)
    pair = np.zeros((GLA_LEVELS + 1, CHUNK, GLA_HEADS, CHUNK), np.float32)
    for lv in range(GLA_LEVELS):
        half = CHUNK >> (lv + 1)
        blk = r // half
        is_upper = blk % 2 == 1
        ref = np.where(is_upper, blk * half - 1, (blk + 1) * half - 1)
        sel[lv, r, ref] = 1.0
        upper[lv, is_upper, :] = 1.0
        pair[lv] = ((r[:, None] // (2 * half)) == (r[None, :] // (2 * half)))[:, None, :]
    pair[GLA_LEVELS] = (r[:, None] == r[None, :])[:, None, :]
    return (jnp.asarray(sel.reshape(GLA_LEVELS * CHUNK, CHUNK), BF16), jnp.asarray(upper),
            jnp.asarray(pair.reshape(GLA_LEVELS + 1, CHUNK, GLA_HEADS * CHUNK)))


def _sgu_gla_kernel(sg_ref, gl_ref, la_ref, sgain_ref, ws_ref, bs_ref, ogain_ref, sel_ref, upper_ref,
                    pair_ref, y_ref, state_ref):
    @pl.when(pl.program_id(1) == 0)
    def _():
        state_ref[...] = jnp.zeros_like(state_ref)

    row = lax.broadcasted_iota(jnp.int32, (CHUNK, CHUNK), 0)
    col = lax.broadcasted_iota(jnp.int32, (CHUNK, CHUNK), 1)
    causal = row >= col
    lane_w = lax.broadcasted_iota(jnp.int32, (CHUNK, SG_WIDTH), 1)

    u = jax.nn.gelu(sg_ref[0, :, :SG_WIDTH])
    v = _rmsnorm(jax.nn.gelu(sg_ref[0, :, SG_WIDTH:]), sgain_ref[...]).astype(BF16)
    mixed = bs_ref[...]
    for g in range(SG_GROUPS):
        wg = jnp.where(causal, ws_ref[g], 0.0).astype(BF16)
        in_group = (lane_w >= g * SG_GROUP_DIM) & (lane_w < (g + 1) * SG_GROUP_DIM)
        mixed = mixed + _dot(wg, jnp.where(in_group, v, jnp.zeros_like(v)))
    y_ref[0, :, :SG_WIDTH] = (u * mixed).astype(BF16)

    kw = GLA_KEY_WIDTH
    q = gl_ref[0, :, :kw] * GLA_DK ** -0.5
    k = gl_ref[0, :, kw:2 * kw]
    vv = gl_ref[0, :, 2 * kw:2 * kw + GLA_WIDTH]
    og = gl_ref[0, :, 2 * kw + GLA_WIDTH:]
    tri = jnp.where(causal, 1.0, 0.0).astype(BF16)
    la = la_ref[0]
    la_hi = la.astype(BF16)
    la_lo = (la - la_hi.astype(F32)).astype(BF16)
    b = _dot(tri, la_hi) + _dot(tri, la_lo)
    b_t = b.T
    k_t = k.T
    b_end_row = b[CHUNK - 1:CHUNK, :]
    b_end_col = b_t[:, CHUNK - 1:CHUNK]

    lane_v = lax.broadcasted_iota(jnp.int32, (CHUNK, GLA_WIDTH), 1)
    lane_k = lax.broadcasted_iota(jnp.int32, (CHUNK, kw), 1)
    state = state_ref[...]
    o = _dot((q * jnp.exp(b)).astype(BF16), state.astype(BF16))
    vv16 = vv.astype(BF16)

    b_hi = b.astype(BF16)
    b_mid = (b - b_hi.astype(F32)).astype(BF16)
    b_lo = (b - b_hi.astype(F32) - b_mid.astype(F32)).astype(BF16)
    b_ref = _dot(sel_ref[...], b_hi) + _dot(sel_ref[...], b_mid) + _dot(sel_ref[...], b_lo)
    key_heads = [(lane_k >= h * GLA_DK) & (lane_k < (h + 1) * GLA_DK) for h in range(GLA_HEADS)]
    scores = jnp.zeros((CHUNK, GLA_HEADS * CHUNK), F32)
    for lv in range(GLA_LEVELS + 1):
        if lv < GLA_LEVELS:
            f = jnp.exp(-jnp.abs(b - b_ref[lv * CHUNK:(lv + 1) * CHUNK, :]))
            ql = (q * f * upper_ref[lv]).astype(BF16)
            kl = (k * f * (1.0 - upper_ref[lv])).astype(BF16)
        else:
            ql, kl = q.astype(BF16), k.astype(BF16)
        kl4 = jnp.concatenate([jnp.where(m, kl, jnp.zeros_like(kl)) for m in key_heads], axis=0)
        scores = scores + _dot_nt(ql, kl4) * pair_ref[lv]
    v4 = jnp.concatenate(
        [jnp.where((lane_v >= h * GLA_DV) & (lane_v < (h + 1) * GLA_DV), vv16, jnp.zeros_like(vv16))
         for h in range(GLA_HEADS)], axis=0)
    o = o + _dot(scores.astype(BF16), v4)

    kd_t = (k_t * jnp.exp(b_end_col - b_t)).astype(BF16)
    row_h = lax.broadcasted_iota(jnp.int32, (kw, GLA_WIDTH), 0) // GLA_DK
    col_h = lax.broadcasted_iota(jnp.int32, (kw, GLA_WIDTH), 1) // GLA_DV
    state_ref[...] = jnp.exp(b_end_col) * state + jnp.where(row_h == col_h, _dot(kd_t, vv16), 0.0)

    r2 = lax.broadcasted_iota(jnp.int32, (GLA_WIDTH, GLA_WIDTH), 0) // GLA_DV
    c2 = lax.broadcasted_iota(jnp.int32, (GLA_WIDTH, GLA_WIDTH), 1) // GLA_DV
    avg = jnp.where(r2 == c2, 1.0 / GLA_DV, 0.0).astype(BF16)
    msq = _split_dot(o * o, avg)
    y = o * lax.rsqrt(msq + EPS) * ogain_ref[...]
    y_ref[0, :, SG_WIDTH:] = (y * (og * (1.0 / (1.0 + jnp.exp(-og))))).astype(BF16)


def _sgu_gla(sg, gl, la, sgain, ws, bs, ogain):
    b, s, _ = sg.shape
    blk = lambda w: pl.BlockSpec((1, CHUNK, w), lambda bi, ci: (bi, ci, 0))
    sel, upper, pair = _gla_level_tables()
    return pl.pallas_call(
        _sgu_gla_kernel,
        grid=(b, s // CHUNK),
        in_specs=[blk(sg.shape[2]), blk(gl.shape[2]), blk(la.shape[2]),
                  _full(sgain.shape), _full(ws.shape), _full(bs.shape), _full(ogain.shape),
                  _full(sel.shape), _full(upper.shape), _full(pair.shape)],
        out_specs=blk(SG_WIDTH + GLA_WIDTH),
        out_shape=jax.ShapeDtypeStruct((b, s, SG_WIDTH + GLA_WIDTH), BF16),
        scratch_shapes=[pltpu.VMEM((GLA_KEY_WIDTH, GLA_WIDTH), F32)],
        compiler_params=_params("parallel", "arbitrary"),
        name="sgu_gla",
    )(sg, gl, la, sgain, ws, bs, ogain, sel, upper, pair)


def _outproj_kernel(x_ref, ysb_ref, yrest_ref, w_ref, o_ref):
    o_ref[...] = (x_ref[...] + _dot(ysb_ref[...], w_ref[:SB_WIDTH, :])
                  + _dot(yrest_ref[...], w_ref[SB_WIDTH:, :]))


def _outproj(x2, ysb, yrest, w, tile):
    n, d = x2.shape
    row = lambda w_: pl.BlockSpec((tile, w_), lambda i: (i, 0))
    return pl.pallas_call(
        _outproj_kernel,
        grid=(n // tile,),
        in_specs=[row(d), row(ysb.shape[1]), row(yrest.shape[1]), _full(w.shape)],
        out_specs=row(d),
        out_shape=jax.ShapeDtypeStruct((n, d), F32),
        compiler_params=_params("parallel"),
        name="outproj",
    )(x2, ysb, yrest, w)


def _mem_kv_kernel(mem_ref, gain_ref, w_ref, kv_ref):
    m = _rmsnorm(mem_ref[0], gain_ref[...]).astype(BF16)
    kv_ref[0] = _dot(m, w_ref[...]).astype(BF16)


def _mem_kv(mem, gain, w):
    b, m, d = mem.shape
    return pl.pallas_call(
        _mem_kv_kernel,
        grid=(b,),
        in_specs=[pl.BlockSpec((1, m, d), lambda bi: (bi, 0, 0)), _full(gain.shape), _full(w.shape)],
        out_specs=pl.BlockSpec((1, m, 2 * d), lambda bi: (bi, 0, 0)),
        out_shape=jax.ShapeDtypeStruct((b, m, 2 * d), BF16),
        compiler_params=_params("parallel"),
        name="mem_kv",
    )(mem, gain, w)


def _cross_kernel(x_ref, gain_ref, wq_ref, kv_ref, wo_ref, o_ref, att_ref):
    x = x_ref[...]
    d = x.shape[1]
    hd = d // X_HEADS
    h = _rmsnorm(x, gain_ref[...]).astype(BF16)
    q = (_dot(h, wq_ref[...]) * hd ** -0.5).astype(BF16)
    for a in range(X_HEADS):
        kh = kv_ref[0, :, a * hd:(a + 1) * hd]
        vh = kv_ref[0, :, d + a * hd:d + (a + 1) * hd]
        s = _dot_nt(q[:, a * hd:(a + 1) * hd], kh)
        e = jnp.exp(s - jnp.max(s, axis=-1, keepdims=True))
        p = e / jnp.sum(e, axis=-1, keepdims=True)
        att_ref[:, a * hd:(a + 1) * hd] = _dot(p.astype(BF16), vh).astype(BF16)
    o_ref[...] = x + _dot(att_ref[...], wo_ref[...])


def _cross(x2, gain, wq, kv, wo, tile, seq):
    n, d = x2.shape
    tiles_per_seq = seq // tile
    row = pl.BlockSpec((tile, d), lambda i: (i, 0))
    return pl.pallas_call(
        _cross_kernel,
        grid=(n // tile,),
        in_specs=[row, _full(gain.shape), _full(wq.shape),
                  pl.BlockSpec((1,) + kv.shape[1:], lambda i: (i // tiles_per_seq, 0, 0)),
                  _full(wo.shape)],
        out_specs=row,
        out_shape=jax.ShapeDtypeStruct((n, d), F32),
        scratch_shapes=[pltpu.VMEM((tile, d), BF16)],
        compiler_params=_params("parallel"),
        name="cross",
    )(x2, gain, wq, kv, wo)


_CAND_PAIRS = [(r1, r2) for r1 in range(PEER_TOPK) for r2 in range(PEER_TOPK)
               if (r1 + 1) * (r2 + 1) <= PEER_TOPK]


def _top16(s):
    work = s
    rank = jnp.full(s.shape, float(PEER_KEYS - 1), F32)
    vals = []
    for r in range(PEER_TOPK):
        m = jnp.max(work, axis=0, keepdims=True)
        hit = work == m
        rank = jnp.where(hit, float(r), rank)
        work = jnp.where(hit, -jnp.inf, work)
        vals.append(m)
    return vals, rank


def _bf16_pair_words(x):
    u = lax.bitcast_convert_type(x.astype(BF16).astype(F32), jnp.uint32)
    return u | (u >> 16)


def _peer_select_kernel(x_ref, gain_ref, wqt_ref, keys_ref, ht_ref, rank2_ref, e2_ref, cnt_ref, e1_ref):
    h_t = _rmsnorm(x_ref[...], gain_ref[...]).T.astype(BF16)
    ht_ref[...] = h_t
    q_t = _dot(wqt_ref[...], h_t).astype(BF16)
    tokens = h_t.shape[1]
    for hd in range(PEER_HEADS):
        base = hd * 2 * PEER_HALF
        s1 = _dot(keys_ref[hd, 0], q_t[base:base + PEER_HALF, :])
        s2 = _dot(keys_ref[hd, 1], q_t[base + PEER_HALF:base + 2 * PEER_HALF, :])
        v1, _ = _top16(s1)
        v2, rank2 = _top16(s2)
        cands = [v1[r1] + v2[r2] for r1, r2 in _CAND_PAIRS]
        pad = -len(cands) % 8
        cands = jnp.concatenate(cands + [jnp.full((pad, tokens), -jnp.inf, F32)], axis=0)
        top = []
        for _ in range(PEER_TOPK):
            m = jnp.max(cands, axis=0, keepdims=True)
            cands = jnp.where(cands == m, -jnp.inf, cands)
            top.append(m)
        tau = top[-1]
        top = jnp.concatenate(top, axis=0)
        z = jnp.sum(jnp.exp(top - top[0:1, :]), axis=0, keepdims=True)
        cnt = jnp.zeros(s1.shape, F32)
        for r2 in range(PEER_TOPK):
            cnt = cnt + jnp.where(s1 + v2[r2] >= tau, 1.0, 0.0)
        rank2_ref[hd] = rank2.astype(BF16)
        e2_ref[hd] = jnp.exp(s2 - v2[0]).astype(BF16)
        cnt_ref[hd] = _bf16_pair_words(cnt)
        e1_ref[hd] = _bf16_pair_words(jnp.exp(s1 - v1[0]) / z)


def _peer_select(x2, gain, wqt, keys, tile):
    n, d = x2.shape
    per_head = lambda: pl.BlockSpec((PEER_HEADS, PEER_KEYS, tile), lambda i: (0, 0, i))
    sel_shape = (PEER_HEADS, PEER_KEYS, n)
    return pl.pallas_call(
        _peer_select_kernel,
        grid=(n // tile,),
        in_specs=[pl.BlockSpec((tile, d), lambda i: (i, 0)), _full(gain.shape), _full(wqt.shape),
                  _full(keys.shape)],
        out_specs=[pl.BlockSpec((d, tile), lambda i: (0, i)),
                   per_head(), per_head(), per_head(), per_head()],
        out_shape=[jax.ShapeDtypeStruct((d, n), BF16),
                   jax.ShapeDtypeStruct(sel_shape, BF16), jax.ShapeDtypeStruct(sel_shape, BF16),
                   jax.ShapeDtypeStruct(sel_shape, jnp.uint32),
                   jax.ShapeDtypeStruct(sel_shape, jnp.uint32)],
        compiler_params=_params("parallel"),
        name="peer_select",
    )(x2, gain, wqt, keys)


def _peer_main_kernel(x_ref, ht_ref, rank2_ref, e2_ref, cnt_ref, e1_ref, u_ref, vt_ref, fgain_ref,
                      o_ref, acc_ref, act0_ref, act1_ref, ga0_ref, ga1_ref, *, final_norm,
                      blocks_per_tile):
    g = pl.program_id(0)
    c_block = jnp.maximum(g - 2, 0)
    tokens = ht_ref.shape[1]
    sub = PEER_KEYS // BF16_ROWS

    @pl.when(g == 0)
    def _():
        for ref in (act0_ref, act1_ref, ga0_ref, ga1_ref):
            ref[...] = jnp.zeros_like(ref)

    @pl.when(c_block % blocks_per_tile == 0)
    def _():
        acc_ref[...] = jnp.zeros_like(acc_ref)

    def pair_row(ref, hd, row):
        words = jnp.broadcast_to(ref[hd, row:row + 1, :], (BF16_ROWS // 2, tokens))
        return pltpu.bitcast(words, BF16)

    def step(act_w, act_r, ga_w, ga_r):
        per_chunk = STAGE_ROWS // PEER_KEYS
        for c in range(u_ref.shape[0] // STAGE_ROWS):
            crows = slice(c * STAGE_ROWS, (c + 1) * STAGE_ROWS)
            act_w[crows, :] = _dot(u_ref[crows, :], ht_ref[...])
            acc_ref[...] += _dot(vt_ref[c], ga_r[crows, :])
            for ii in range(c * per_chunk, (c + 1) * per_chunk):
                gate = jnp.zeros((sub, BF16_ROWS, tokens), BF16)
                for hd in range(PEER_HEADS):
                    w = e2_ref[hd] * pair_row(e1_ref, hd, ii)[None]
                    hit = rank2_ref[hd] < pair_row(cnt_ref, hd, ii)[None]
                    gate = gate + jnp.where(hit, w, jnp.zeros_like(w))
                rows = slice(ii * PEER_KEYS, (ii + 1) * PEER_KEYS)
                a = act_r[rows, :]
                gl = a / (1.0 + jnp.exp2(a * (GELU_K1 + GELU_K3 * (a * a))))
                ga_w[rows, :] = (gl.astype(BF16).reshape(sub, BF16_ROWS, tokens) * gate
                                 ).reshape(PEER_KEYS, tokens)

    @pl.when(g % 2 == 0)
    def _():
        step(act0_ref, act1_ref, ga0_ref, ga1_ref)

    @pl.when(g % 2 == 1)
    def _():
        step(act1_ref, act0_ref, ga1_ref, ga0_ref)

    @pl.when((g >= 2) & (c_block % blocks_per_tile == blocks_per_tile - 1))
    def _():
        y = x_ref[...] + acc_ref[...].T
        if final_norm:
            y = _rmsnorm(y, fgain_ref[...])
        o_ref[...] = y


def _peer_main(x2, ht, rank2, e2, cnt, e1, u16, vt16, fgain, tile, etile, final_norm):
    n, d = x2.shape
    n_exp = u16.shape[0]
    bpt = n_exp // etile
    last = (n // tile) * bpt - 1
    a_blk = lambda g: jnp.minimum(g, last)
    b_blk = lambda g: jnp.clip(g - 1, 0, last)
    c_blk = lambda g: jnp.clip(g - 2, 0, last)
    rows = etile // PEER_KEYS
    packed = lambda: pl.BlockSpec((PEER_HEADS, PEER_KEYS // BF16_ROWS, BF16_ROWS, tile),
                                  lambda g: (0, 0, 0, b_blk(g) // bpt))
    row_words = lambda: pl.BlockSpec((PEER_HEADS, rows, tile),
                                     lambda g: (0, b_blk(g) % bpt, b_blk(g) // bpt))
    split_rows = lambda a: a.reshape(PEER_HEADS, PEER_KEYS // BF16_ROWS, BF16_ROWS, n)
    return pl.pallas_call(
        functools.partial(_peer_main_kernel, final_norm=final_norm, blocks_per_tile=bpt),
        grid=(last + 3,),
        in_specs=[pl.BlockSpec((tile, d), lambda g: (c_blk(g) // bpt, 0)),
                  pl.BlockSpec((d, tile), lambda g: (0, a_blk(g) // bpt)),
                  packed(), packed(), row_words(), row_words(),
                  pl.BlockSpec((etile, d), lambda g: (a_blk(g) % bpt, 0)),
                  pl.BlockSpec((etile // STAGE_ROWS, d, STAGE_ROWS), lambda g: (c_blk(g) % bpt, 0, 0)),
                  pl.BlockSpec(fgain.shape, lambda g: (0, 0))],
        out_specs=pl.BlockSpec((tile, d), lambda g: (c_blk(g) // bpt, 0)),
        out_shape=jax.ShapeDtypeStruct((n, d), F32),
        scratch_shapes=[pltpu.VMEM((d, tile), F32),
                        pltpu.VMEM((etile, tile), F32), pltpu.VMEM((etile, tile), F32),
                        pltpu.VMEM((etile, tile), BF16), pltpu.VMEM((etile, tile), BF16)],
        compiler_params=_params("arbitrary"),
        name="peer_main",
    )(x2, ht, split_rows(rank2), split_rows(e2), cnt, e1, u16, vt16, fgain)


def _tile(n, want):
    t = min(want, n)
    while n % t:
        t //= 2
    return t


def kernel(x, mem, norm_mix, w_in, sg_v_gain, sg_w_spatial, sg_b_spatial, gla_w_gate, gla_b_gate,
           gla_out_gain, w_out, norm_mem, mem_gain, w_cq, w_ckv, w_co, norm_ffn, peer_w_q,
           peer_sub_keys, peer_u, peer_v, final_gain):
    b, s, d = x.shape
    n = b * s
    depth = w_in.shape[0]
    row_tile = _tile(s, 512)
    sb_tile = _tile(s, 512)
    sel_tile = _tile(n, 256)
    main_tile = _tile(n, 512)
    etile = 1024

    sb_end = 3 * SB_WIDTH
    sg_end = sb_end + 2 * SG_WIDTH
    gl_end = sg_end + 2 * GLA_KEY_WIDTH + 2 * GLA_WIDTH
    x2 = x.reshape(n, d)
    for l in range(depth):
        w16 = w_in[l].astype(BF16)
        wga = jnp.pad(w16[:, gl_end:], ((0, 0), (0, LANES - GLA_GATE_RANK)))
        wgate = jnp.pad(gla_w_gate[l].astype(BF16), ((0, LANES - GLA_GATE_RANK), (0, 0)))
        sb, sg, gl, la = _inproj(x2, norm_mix[l][None], w16[:, :sb_end], w16[:, sb_end:sg_end],
                                 w16[:, sg_end:gl_end], wga, wgate, gla_b_gate[l][None], row_tile)
        ysb = _sb_attn(sb.reshape(b, s, -1), sb_tile)
        bias = jnp.repeat(sg_b_spatial[l].T, SG_GROUP_DIM, axis=1)
        yrest = _sgu_gla(sg.reshape(b, s, -1), gl.reshape(b, s, -1), la.reshape(b, s, -1),
                         sg_v_gain[l][None], sg_w_spatial[l], bias, gla_out_gain[l][None])
        x2 = _outproj(x2, ysb.reshape(n, -1), yrest.reshape(n, -1), w_out[l].astype(BF16), row_tile)

        kv = _mem_kv(mem, mem_gain[l][None], w_ckv[l].astype(BF16))
        x2 = _cross(x2, norm_mem[l][None], w_cq[l].astype(BF16), kv, w_co[l].astype(BF16),
                    row_tile, s)

        keys = peer_sub_keys[l].astype(BF16)
        ht, rank2, e2, cnt, e1 = _peer_select(x2, norm_ffn[l][None], peer_w_q[l].T.astype(BF16),
                                              keys, sel_tile)
        vt = peer_v[l].astype(BF16).reshape(-1, STAGE_ROWS, d).transpose(0, 2, 1)
        x2 = _peer_main(x2, ht, rank2, e2, cnt, e1, peer_u[l].astype(BF16),
                        vt, final_gain[None], main_tile, etile,
                        final_norm=(l == depth - 1))
    return x2.reshape(b, s, d)
```

```python
import functools

import jax
import jax.numpy as jnp
import numpy as np
from jax import lax
from jax.experimental import pallas as pl
from jax.experimental.pallas import tpu as pltpu

EPS = 1e-6
LOG2E = 1.4426950408889634

LANES = 128
BF16_ROWS = 16
GELU_K1 = -2.0 * 0.7978845608028654 * LOG2E
GELU_K3 = GELU_K1 * 0.044715
VMEM_LIMIT_BYTES = 56 * 1024 * 1024

SB_HEADS = 8
SB_HEAD_DIM = 64
SB_WIDTH = SB_HEADS * SB_HEAD_DIM
SG_GROUPS = 4
SG_GROUP_DIM = 64
SG_WIDTH = SG_GROUPS * SG_GROUP_DIM
GLA_HEADS = 4
GLA_DK = 32
GLA_DV = 64
GLA_KEY_WIDTH = GLA_HEADS * GLA_DK
GLA_WIDTH = GLA_HEADS * GLA_DV
GLA_GATE_RANK = 16
GLA_TAU = 16.0
CHUNK = 128
X_HEADS = 4
PEER_HEADS = 8
PEER_KEYS = 128
PEER_TOPK = 16
PEER_HALF = 64
SB_UNROLL = 4
STAGE_ROWS = 256

BF16 = jnp.bfloat16
F32 = jnp.float32


def _params(*semantics, flags=None):
    return pltpu.CompilerParams(dimension_semantics=semantics,
                                vmem_limit_bytes=VMEM_LIMIT_BYTES, flags=flags)


def _dot(a, b):
    return jnp.dot(a, b, preferred_element_type=F32)


def _dot_nt(a, b):
    return lax.dot_general(a, b, (((1,), (1,)), ((), ())), preferred_element_type=F32)


def _split_dot(a, b16):
    hi = a.astype(BF16)
    lo = (a - hi.astype(F32)).astype(BF16)
    return _dot(hi, b16) + _dot(lo, b16)


def _rmsnorm(x, gain):
    return x * lax.rsqrt(jnp.mean(x * x, axis=-1, keepdims=True) + EPS) * gain


def _log_sigmoid(x):
    return jnp.minimum(x, 0.0) - jnp.log1p(jnp.exp(-jnp.abs(x)))


def _full(shape):
    n = len(shape)
    return pl.BlockSpec(shape, lambda *_: (0,) * n)


def _inproj_kernel(x_ref, gain_ref, wsb_ref, wsg_ref, wgl_ref, wga_ref, wgate_ref, bgate_ref,
                   sb_ref, sg_ref, gl_ref, la_ref):
    h = _rmsnorm(x_ref[...], gain_ref[...]).astype(BF16)
    sb = _dot(h, wsb_ref[...])
    sb_ref[:, :SB_WIDTH] = (sb[:, :SB_WIDTH] * (SB_HEAD_DIM ** -0.5 * LOG2E)).astype(BF16)
    sb_ref[:, SB_WIDTH:] = sb[:, SB_WIDTH:].astype(BF16)
    sg_ref[...] = _dot(h, wsg_ref[...])
    gl_ref[...] = _dot(h, wgl_ref[...])
    ga = _dot(h, wga_ref[...])
    gate = _dot(ga.astype(BF16), wgate_ref[...]) + bgate_ref[...]
    la_ref[...] = _log_sigmoid(gate) * (1.0 / GLA_TAU)


def _inproj(x2, gain, wsb, wsg, wgl, wga, wgate, bgate, tile):
    n, d = x2.shape
    row = lambda w: pl.BlockSpec((tile, w), lambda i: (i, 0))
    return pl.pallas_call(
        _inproj_kernel,
        grid=(n // tile,),
        in_specs=[row(d), _full(gain.shape), _full(wsb.shape), _full(wsg.shape), _full(wgl.shape),
                  _full(wga.shape), _full(wgate.shape), _full(bgate.shape)],
        out_specs=[row(3 * SB_WIDTH), row(2 * SG_WIDTH), row(2 * GLA_KEY_WIDTH + 2 * GLA_WIDTH),
                   row(GLA_KEY_WIDTH)],
        out_shape=[jax.ShapeDtypeStruct((n, 3 * SB_WIDTH), BF16),
                   jax.ShapeDtypeStruct((n, 2 * SG_WIDTH), F32),
                   jax.ShapeDtypeStruct((n, 2 * GLA_KEY_WIDTH + 2 * GLA_WIDTH), F32),
                   jax.ShapeDtypeStruct((n, GLA_KEY_WIDTH), F32)],
        compiler_params=_params("parallel"),
        name="inproj",
    )(x2, gain, wsb, wsg, wgl, wga, wgate, bgate)


def _sb_kernel(q_ref, k_ref, v_ref, o_ref, acc_ref, carry_ref, *, tq, split):
    qi = pl.program_id(2)
    nsub = tq // CHUNK
    lane = lax.broadcasted_iota(jnp.int32, (CHUNK, LANES), 1)
    head0 = lane < SB_HEAD_DIM
    r_i = lax.broadcasted_iota(jnp.int32, (2 * LANES, 2 * LANES), 0)
    c_i = lax.broadcasted_iota(jnp.int32, (2 * LANES, 2 * LANES), 1)
    m_incl = jnp.where((r_i >= c_i) & ((r_i >= LANES) == (c_i >= LANES)), -1.0, 0.0).astype(BF16)

    acc_ref[...] = jnp.zeros_like(acc_ref)
    carry_ref[...] = jnp.zeros_like(carry_ref)

    def blocks(kjs, row0, masked):
        rows = slice(row0, tq)
        nrow = tq - row0
        q = q_ref[0, rows, :]
        carry = carry_ref[rows, :]
        acc = acc_ref[rows, :]
        for kj in kjs:
            start = pl.multiple_of(kj * CHUNK, CHUNK)
            kb = k_ref[0, pl.ds(start, CHUNK), :]
            vb = v_ref[0, pl.ds(start, CHUNK), :]
            zero = jnp.zeros_like(kb)
            k2 = jnp.concatenate([jnp.where(head0, kb, zero), jnp.where(head0, zero, kb)], axis=0)
            v2 = jnp.concatenate([jnp.where(head0, vb, zero), jnp.where(head0, zero, vb)], axis=0)
            z = _dot_nt(q, k2)
            nl = jnp.maximum(z, 0.0) + jnp.log(1.0 + jnp.exp2(-jnp.abs(z))) * LOG2E
            if masked:
                t_loc = row0 + lax.broadcasted_iota(jnp.int32, (nrow, 2 * LANES), 0)
                s_loc = row0 + (lax.broadcasted_iota(jnp.int32, (nrow, 2 * LANES), 1) & (LANES - 1))
                mask = s_loc < t_loc
                nl = jnp.where(mask, nl, 0.0)
            incl = _split_dot(nl, m_incl) if split else _dot(nl.astype(BF16), m_incl)
            w = jnp.exp2(z + incl + carry)
            if masked:
                w = jnp.where(mask, w, 0.0)
            acc = acc + _dot(w.astype(BF16), v2)
            carry = carry + jnp.concatenate(
                [jnp.broadcast_to(incl[:, 0:1], (nrow, LANES)),
                 jnp.broadcast_to(incl[:, LANES:LANES + 1], (nrow, LANES))], axis=1)
        carry_ref[rows, :] = carry
        acc_ref[rows, :] = acc

    for c in reversed(range(nsub)):
        blocks([qi * nsub + c], c * CHUNK, True)

    def body(it, _):
        first = qi * nsub - 1 - it * SB_UNROLL
        blocks([first - u for u in range(SB_UNROLL)], 0, False)
        return 0

    lax.fori_loop(0, qi * (nsub // SB_UNROLL), body, 0)
    o_ref[0] = acc_ref[...].astype(BF16)


def _sb_attn(sb, tq, split):
    b, s, _ = sb.shape
    pairs = SB_WIDTH // LANES
    return pl.pallas_call(
        functools.partial(_sb_kernel, tq=tq, split=split),
        grid=(b, pairs, s // tq),
        in_specs=[pl.BlockSpec((1, tq, LANES), lambda bi, hp, qi: (bi, qi, hp)),
                  pl.BlockSpec((1, s, LANES), lambda bi, hp, qi: (bi, 0, pairs + hp)),
                  pl.BlockSpec((1, s, LANES), lambda bi, hp, qi: (bi, 0, 2 * pairs + hp))],
        out_specs=pl.BlockSpec((1, tq, LANES), lambda bi, hp, qi: (bi, qi, hp)),
        out_shape=jax.ShapeDtypeStruct((b, s, SB_WIDTH), BF16),
        scratch_shapes=[pltpu.VMEM((tq, LANES), F32), pltpu.VMEM((tq, 2 * LANES), F32)],
        compiler_params=_params("parallel", "parallel", "parallel"),
        name="sb_attn",
    )(sb, sb, sb)


GLA_LEVELS = 7


def _gla_level_tables():
    r = np.arange(CHUNK)
    sel = np.zeros((GLA_LEVELS, CHUNK, CHUNK), np.float32)
    upper = np.zeros((GLA_LEVELS, CHUNK, GLA_KEY_WIDTH), np.float32)
    pair = np.zeros((GLA_LEVELS + 1, CHUNK, GLA_HEADS, CHUNK), np.float32)
    for lv in range(GLA_LEVELS):
        half = CHUNK >> (lv + 1)
        blk = r // half
        is_upper = blk % 2 == 1
        ref = np.where(is_upper, blk * half - 1, (blk + 1) * half - 1)
        sel[lv, r, ref] = 1.0
        upper[lv, is_upper, :] = 1.0
        pair[lv] = ((r[:, None] // (2 * half)) == (r[None, :] // (2 * half)))[:, None, :]
    pair[GLA_LEVELS] = (r[:, None] == r[None, :])[:, None, :]
    return (jnp.asarray(sel.reshape(GLA_LEVELS * CHUNK, CHUNK), BF16), jnp.asarray(upper),
            jnp.asarray(pair.reshape(GLA_LEVELS + 1, CHUNK, GLA_HEADS * CHUNK)))


def _sgu_gla_kernel(sg_ref, gl_ref, la_ref, sgain_ref, ws_ref, bs_ref, ogain_ref, sel_ref, upper_ref,
                    pair_ref, y_ref, state_ref):
    @pl.when(pl.program_id(1) == 0)
    def _():
        state_ref[...] = jnp.zeros_like(state_ref)

    row = lax.broadcasted_iota(jnp.int32, (CHUNK, CHUNK), 0)
    col = lax.broadcasted_iota(jnp.int32, (CHUNK, CHUNK), 1)
    causal = row >= col
    lane_w = lax.broadcasted_iota(jnp.int32, (CHUNK, SG_WIDTH), 1)

    u = jax.nn.gelu(sg_ref[0, :, :SG_WIDTH])
    v = _rmsnorm(jax.nn.gelu(sg_ref[0, :, SG_WIDTH:]), sgain_ref[...]).astype(BF16)
    mixed = bs_ref[...]
    for g in range(SG_GROUPS):
        wg = jnp.where(causal, ws_ref[g], 0.0).astype(BF16)
        in_group = (lane_w >= g * SG_GROUP_DIM) & (lane_w < (g + 1) * SG_GROUP_DIM)
        mixed = mixed + _dot(wg, jnp.where(in_group, v, jnp.zeros_like(v)))
    y_ref[0, :, :SG_WIDTH] = (u * mixed).astype(BF16)

    kw = GLA_KEY_WIDTH
    q = gl_ref[0, :, :kw] * GLA_DK ** -0.5
    k = gl_ref[0, :, kw:2 * kw]
    vv = gl_ref[0, :, 2 * kw:2 * kw + GLA_WIDTH]
    og = gl_ref[0, :, 2 * kw + GLA_WIDTH:]
    tri = jnp.where(causal, 1.0, 0.0).astype(BF16)
    la = la_ref[0]
    la_hi = la.astype(BF16)
    la_lo = (la - la_hi.astype(F32)).astype(BF16)
    b = _dot(tri, la_hi) + _dot(tri, la_lo)
    b_t = b.T
    k_t = k.T
    b_end_row = b[CHUNK - 1:CHUNK, :]
    b_end_col = b_t[:, CHUNK - 1:CHUNK]

    lane_v = lax.broadcasted_iota(jnp.int32, (CHUNK, GLA_WIDTH), 1)
    lane_k = lax.broadcasted_iota(jnp.int32, (CHUNK, kw), 1)
    state = state_ref[...]
    o = _dot((q * jnp.exp(b)).astype(BF16), state.astype(BF16))
    vv16 = vv.astype(BF16)

    b_hi = b.astype(BF16)
    b_mid = (b - b_hi.astype(F32)).astype(BF16)
    b_lo = (b - b_hi.astype(F32) - b_mid.astype(F32)).astype(BF16)
    b_ref = _dot(sel_ref[...], b_hi) + _dot(sel_ref[...], b_mid) + _dot(sel_ref[...], b_lo)
    key_heads = [(lane_k >= h * GLA_DK) & (lane_k < (h + 1) * GLA_DK) for h in range(GLA_HEADS)]
    scores = jnp.zeros((CHUNK, GLA_HEADS * CHUNK), F32)
    for lv in range(GLA_LEVELS + 1):
        if lv < GLA_LEVELS:
            f = jnp.exp(-jnp.abs(b - b_ref[lv * CHUNK:(lv + 1) * CHUNK, :]))
            ql = (q * f * upper_ref[lv]).astype(BF16)
            kl = (k * f * (1.0 - upper_ref[lv])).astype(BF16)
        else:
            ql, kl = q.astype(BF16), k.astype(BF16)
        kl4 = jnp.concatenate([jnp.where(m, kl, jnp.zeros_like(kl)) for m in key_heads], axis=0)
        scores = scores + _dot_nt(ql, kl4) * pair_ref[lv]
    v4 = jnp.concatenate(
        [jnp.where((lane_v >= h * GLA_DV) & (lane_v < (h + 1) * GLA_DV), vv16, jnp.zeros_like(vv16))
         for h in range(GLA_HEADS)], axis=0)
    o = o + _dot(scores.astype(BF16), v4)

    kd_t = (k_t * jnp.exp(b_end_col - b_t)).astype(BF16)
    row_h = lax.broadcasted_iota(jnp.int32, (kw, GLA_WIDTH), 0) // GLA_DK
    col_h = lax.broadcasted_iota(jnp.int32, (kw, GLA_WIDTH), 1) // GLA_DV
    state_ref[...] = jnp.exp(b_end_col) * state + jnp.where(row_h == col_h, _dot(kd_t, vv16), 0.0)

    r2 = lax.broadcasted_iota(jnp.int32, (GLA_WIDTH, GLA_WIDTH), 0) // GLA_DV
    c2 = lax.broadcasted_iota(jnp.int32, (GLA_WIDTH, GLA_WIDTH), 1) // GLA_DV
    avg = jnp.where(r2 == c2, 1.0 / GLA_DV, 0.0).astype(BF16)
    msq = _split_dot(o * o, avg)
    y = o * lax.rsqrt(msq + EPS) * ogain_ref[...]
    y_ref[0, :, SG_WIDTH:] = (y * (og * (1.0 / (1.0 + jnp.exp(-og))))).astype(BF16)


def _sgu_gla(sg, gl, la, sgain, ws, bs, ogain):
    b, s, _ = sg.shape
    blk = lambda w: pl.BlockSpec((1, CHUNK, w), lambda bi, ci: (bi, ci, 0))
    sel, upper, pair = _gla_level_tables()
    return pl.pallas_call(
        _sgu_gla_kernel,
        grid=(b, s // CHUNK),
        in_specs=[blk(sg.shape[2]), blk(gl.shape[2]), blk(la.shape[2]),
                  _full(sgain.shape), _full(ws.shape), _full(bs.shape), _full(ogain.shape),
                  _full(sel.shape), _full(upper.shape), _full(pair.shape)],
        out_specs=blk(SG_WIDTH + GLA_WIDTH),
        out_shape=jax.ShapeDtypeStruct((b, s, SG_WIDTH + GLA_WIDTH), BF16),
        scratch_shapes=[pltpu.VMEM((GLA_KEY_WIDTH, GLA_WIDTH), F32)],
        compiler_params=_params("parallel", "arbitrary"),
        name="sgu_gla",
    )(sg, gl, la, sgain, ws, bs, ogain, sel, upper, pair)


def _outproj_kernel(x_ref, ysb_ref, yrest_ref, w_ref, o_ref):
    o_ref[...] = (x_ref[...] + _dot(ysb_ref[...], w_ref[:SB_WIDTH, :])
                  + _dot(yrest_ref[...], w_ref[SB_WIDTH:, :]))


def _outproj(x2, ysb, yrest, w, tile):
    n, d = x2.shape
    row = lambda w_: pl.BlockSpec((tile, w_), lambda i: (i, 0))
    return pl.pallas_call(
        _outproj_kernel,
        grid=(n // tile,),
        in_specs=[row(d), row(ysb.shape[1]), row(yrest.shape[1]), _full(w.shape)],
        out_specs=row(d),
        out_shape=jax.ShapeDtypeStruct((n, d), F32),
        compiler_params=_params("parallel"),
        name="outproj",
    )(x2, ysb, yrest, w)


def _mem_kv_kernel(mem_ref, gain_ref, w_ref, kv_ref):
    m = _rmsnorm(mem_ref[0], gain_ref[...]).astype(BF16)
    kv_ref[0] = _dot(m, w_ref[...]).astype(BF16)


def _mem_kv(mem, gain, w):
    b, m, d = mem.shape
    return pl.pallas_call(
        _mem_kv_kernel,
        grid=(b,),
        in_specs=[pl.BlockSpec((1, m, d), lambda bi: (bi, 0, 0)), _full(gain.shape), _full(w.shape)],
        out_specs=pl.BlockSpec((1, m, 2 * d), lambda bi: (bi, 0, 0)),
        out_shape=jax.ShapeDtypeStruct((b, m, 2 * d), BF16),
        compiler_params=_params("parallel"),
        name="mem_kv",
    )(mem, gain, w)


def _cross_kernel(x_ref, gain_ref, wq_ref, kv_ref, wo_ref, o_ref, att_ref):
    x = x_ref[...]
    d = x.shape[1]
    hd = d // X_HEADS
    h = _rmsnorm(x, gain_ref[...]).astype(BF16)
    q = (_dot(h, wq_ref[...]) * hd ** -0.5).astype(BF16)
    for a in range(X_HEADS):
        kh = kv_ref[0, :, a * hd:(a + 1) * hd]
        vh = kv_ref[0, :, d + a * hd:d + (a + 1) * hd]
        s = _dot_nt(q[:, a * hd:(a + 1) * hd], kh)
        e = jnp.exp(s - jnp.max(s, axis=-1, keepdims=True))
        p = e / jnp.sum(e, axis=-1, keepdims=True)
        att_ref[:, a * hd:(a + 1) * hd] = _dot(p.astype(BF16), vh).astype(BF16)
    o_ref[...] = x + _dot(att_ref[...], wo_ref[...])


def _cross(x2, gain, wq, kv, wo, tile, seq):
    n, d = x2.shape
    tiles_per_seq = seq // tile
    row = pl.BlockSpec((tile, d), lambda i: (i, 0))
    return pl.pallas_call(
        _cross_kernel,
        grid=(n // tile,),
        in_specs=[row, _full(gain.shape), _full(wq.shape),
                  pl.BlockSpec((1,) + kv.shape[1:], lambda i: (i // tiles_per_seq, 0, 0)),
                  _full(wo.shape)],
        out_specs=row,
        out_shape=jax.ShapeDtypeStruct((n, d), F32),
        scratch_shapes=[pltpu.VMEM((tile, d), BF16)],
        compiler_params=_params("parallel"),
        name="cross",
    )(x2, gain, wq, kv, wo)


_CAND_PAIRS = [(r1, r2) for r1 in range(PEER_TOPK) for r2 in range(PEER_TOPK)
               if (r1 + 1) * (r2 + 1) <= PEER_TOPK]


def _top16(s):
    work = s
    rank = jnp.full(s.shape, float(PEER_KEYS - 1), F32)
    vals = []
    for r in range(PEER_TOPK):
        m = jnp.max(work, axis=0, keepdims=True)
        hit = work == m
        rank = jnp.where(hit, float(r), rank)
        work = jnp.where(hit, -jnp.inf, work)
        vals.append(m)
    return vals, rank


def _bf16_pair_words(x):
    u = lax.bitcast_convert_type(x.astype(BF16).astype(F32), jnp.uint32)
    return u | (u >> 16)


def _peer_select_kernel(x_ref, gain_ref, wqt_ref, keys_ref, ht_ref, rank2_ref, e2_ref, cnt_ref, e1_ref,
                        qt_ref, *, looped):
    h_t = _rmsnorm(x_ref[...], gain_ref[...]).T.astype(BF16)
    ht_ref[...] = h_t
    qt_ref[...] = _dot(wqt_ref[...], h_t).astype(BF16)
    tokens = h_t.shape[1]

    def head(hd):
        base = hd * 2 * PEER_HALF
        if looped:
            base = pl.multiple_of(base, 2 * PEER_HALF)
        s1 = _dot(keys_ref[hd, 0], qt_ref[pl.ds(base, PEER_HALF), :])
        s2 = _dot(keys_ref[hd, 1], qt_ref[pl.ds(base + PEER_HALF, PEER_HALF), :])
        v1, _ = _top16(s1)
        v2, rank2 = _top16(s2)
        cands = [v1[r1] + v2[r2] for r1, r2 in _CAND_PAIRS]
        pad = -len(cands) % 8
        cands = jnp.concatenate(cands + [jnp.full((pad, tokens), -jnp.inf, F32)], axis=0)
        top = []
        for _ in range(PEER_TOPK):
            m = jnp.max(cands, axis=0, keepdims=True)
            cands = jnp.where(cands == m, -jnp.inf, cands)
            top.append(m)
        tau = top[-1]
        top = jnp.concatenate(top, axis=0)
        z = jnp.sum(jnp.exp(top - top[0:1, :]), axis=0, keepdims=True)
        cnt = jnp.zeros(s1.shape, F32)
        for r2 in range(PEER_TOPK):
            cnt = jnp.where(s1 + v2[r2] >= tau, float(r2 + 1), cnt)
        rank2_ref[hd] = rank2.astype(BF16)
        e2_ref[hd] = jnp.exp(s2 - v2[0]).astype(BF16)
        cnt_ref[hd] = _bf16_pair_words(cnt)
        e1_ref[hd] = _bf16_pair_words(jnp.exp(s1 - v1[0]) / z)

    if looped:
        def body(hd, carry):
            head(hd)
            return carry
        lax.fori_loop(0, PEER_HEADS, body, 0)
    else:
        for hd in range(PEER_HEADS):
            head(hd)


def _peer_select(x2, gain, wqt, keys, tile, looped):
    n, d = x2.shape
    per_head = lambda: pl.BlockSpec((PEER_HEADS, PEER_KEYS, tile), lambda i: (0, 0, i))
    sel_shape = (PEER_HEADS, PEER_KEYS, n)
    return pl.pallas_call(
        functools.partial(_peer_select_kernel, looped=looped),
        grid=(n // tile,),
        in_specs=[pl.BlockSpec((tile, d), lambda i: (i, 0)), _full(gain.shape), _full(wqt.shape),
                  _full(keys.shape)],
        out_specs=[pl.BlockSpec((d, tile), lambda i: (0, i)),
                   per_head(), per_head(), per_head(), per_head()],
        out_shape=[jax.ShapeDtypeStruct((d, n), BF16),
                   jax.ShapeDtypeStruct(sel_shape, BF16), jax.ShapeDtypeStruct(sel_shape, BF16),
                   jax.ShapeDtypeStruct(sel_shape, jnp.uint32),
                   jax.ShapeDtypeStruct(sel_shape, jnp.uint32)],
        scratch_shapes=[pltpu.VMEM((wqt.shape[0], tile), BF16)],
        compiler_params=_params("parallel"),
        name="peer_select",
    )(x2, gain, wqt, keys)


def _peer_main_kernel(x_ref, ht_ref, rank2_ref, e2_ref, cnt_ref, e1_ref, u_ref, vt_ref, fgain_ref,
                      o_ref, acc_ref, act0_ref, act1_ref, ga0_ref, ga1_ref, *, final_norm,
                      blocks_per_tile):
    g = pl.program_id(0)
    c_block = jnp.maximum(g - 2, 0)
    tokens = ht_ref.shape[1]
    sub = PEER_KEYS // BF16_ROWS

    @pl.when(g == 0)
    def _():
        for ref in (act0_ref, act1_ref, ga0_ref, ga1_ref):
            ref[...] = jnp.zeros_like(ref)

    @pl.when(c_block % blocks_per_tile == 0)
    def _():
        acc_ref[...] = jnp.zeros_like(acc_ref)

    def pair_row(ref, hd, row):
        words = jnp.broadcast_to(ref[hd, row:row + 1, :], (BF16_ROWS // 2, tokens))
        return pltpu.bitcast(words, BF16)

    def step(act_w, act_r, ga_w, ga_r):
        acc_ref[...] += _dot(vt_ref[...], ga_r[...])
        per_chunk = STAGE_ROWS // PEER_KEYS
        for c in range(u_ref.shape[0] // STAGE_ROWS):
            crows = slice(c * STAGE_ROWS, (c + 1) * STAGE_ROWS)
            act_w[crows, :] = _dot(u_ref[crows, :], ht_ref[...])
            for ii in range(c * per_chunk, (c + 1) * per_chunk):
                gate = jnp.zeros((sub, BF16_ROWS, tokens), BF16)
                for hd in range(PEER_HEADS):
                    w = e2_ref[hd] * pair_row(e1_ref, hd, ii)[None]
                    hit = rank2_ref[hd] < pair_row(cnt_ref, hd, ii)[None]
                    gate = gate + jnp.where(hit, w, jnp.zeros_like(w))
                rows = slice(ii * PEER_KEYS, (ii + 1) * PEER_KEYS)
                a = act_r[rows, :]
                gl = a / (1.0 + jnp.exp2(a * (GELU_K1 + GELU_K3 * (a * a))))
                ga_w[rows, :] = (gl.astype(BF16).reshape(sub, BF16_ROWS, tokens) * gate
                                 ).reshape(PEER_KEYS, tokens)

    @pl.when(g % 2 == 0)
    def _():
        step(act0_ref, act1_ref, ga0_ref, ga1_ref)

    @pl.when(g % 2 == 1)
    def _():
        step(act1_ref, act0_ref, ga1_ref, ga0_ref)

    @pl.when((g >= 2) & (c_block % blocks_per_tile == blocks_per_tile - 1))
    def _():
        y = x_ref[...] + acc_ref[...].T
        if final_norm:
            y = _rmsnorm(y, fgain_ref[...])
        o_ref[...] = y


def _peer_main(x2, ht, rank2, e2, cnt, e1, u16, vt16, fgain, tile, etile, final_norm):
    n, d = x2.shape
    n_exp = u16.shape[0]
    bpt = n_exp // etile
    last = (n // tile) * bpt - 1
    a_blk = lambda g: jnp.minimum(g, last)
    b_blk = lambda g: jnp.clip(g - 1, 0, last)
    c_blk = lambda g: jnp.clip(g - 2, 0, last)
    rows = etile // PEER_KEYS
    packed = lambda: pl.BlockSpec((PEER_HEADS, PEER_KEYS // BF16_ROWS, BF16_ROWS, tile),
                                  lambda g: (0, 0, 0, b_blk(g) // bpt))
    row_words = lambda: pl.BlockSpec((PEER_HEADS, rows, tile),
                                     lambda g: (0, b_blk(g) % bpt, b_blk(g) // bpt))
    split_rows = lambda a: a.reshape(PEER_HEADS, PEER_KEYS // BF16_ROWS, BF16_ROWS, n)
    return pl.pallas_call(
        functools.partial(_peer_main_kernel, final_norm=final_norm, blocks_per_tile=bpt),
        grid=(last + 3,),
        in_specs=[pl.BlockSpec((tile, d), lambda g: (c_blk(g) // bpt, 0)),
                  pl.BlockSpec((d, tile), lambda g: (0, a_blk(g) // bpt)),
                  packed(), packed(), row_words(), row_words(),
                  pl.BlockSpec((etile, d), lambda g: (a_blk(g) % bpt, 0)),
                  pl.BlockSpec((d, etile), lambda g: (0, c_blk(g) % bpt)),
                  pl.BlockSpec(fgain.shape, lambda g: (0, 0))],
        out_specs=pl.BlockSpec((tile, d), lambda g: (c_blk(g) // bpt, 0)),
        out_shape=jax.ShapeDtypeStruct((n, d), F32),
        scratch_shapes=[pltpu.VMEM((d, tile), F32),
                        pltpu.VMEM((etile, tile), F32), pltpu.VMEM((etile, tile), F32),
                        pltpu.VMEM((etile, tile), BF16), pltpu.VMEM((etile, tile), BF16)],
        compiler_params=_params("arbitrary"),
        name="peer_main",
    )(x2, ht, split_rows(rank2), split_rows(e2), cnt, e1, u16, vt16, fgain)


def _peer_flat_kernel(x_ref, ht_ref, rank2_ref, e2_ref, cnt_ref, e1_ref, u_ref, vt_ref, fgain_ref,
                      o_ref, acc_ref, act_ref, ga_ref, *, final_norm):
    j = pl.program_id(1)
    tokens = ht_ref.shape[1]
    sub = PEER_KEYS // BF16_ROWS

    @pl.when(j == 0)
    def _():
        acc_ref[...] = jnp.zeros_like(acc_ref)

    def pair_row(ref, hd, row):
        words = jnp.broadcast_to(ref[hd, row:row + 1, :], (BF16_ROWS // 2, tokens))
        return pltpu.bitcast(words, BF16)

    act_ref[...] = _dot(u_ref[...], ht_ref[...])
    for ii in range(u_ref.shape[0] // PEER_KEYS):
        gate = jnp.zeros((sub, BF16_ROWS, tokens), BF16)
        for hd in range(PEER_HEADS):
            w = e2_ref[hd] * pair_row(e1_ref, hd, ii)[None]
            hit = rank2_ref[hd] < pair_row(cnt_ref, hd, ii)[None]
            gate = gate + jnp.where(hit, w, jnp.zeros_like(w))
        rows = slice(ii * PEER_KEYS, (ii + 1) * PEER_KEYS)
        a = act_ref[rows, :]
        gl = a / (1.0 + jnp.exp2(a * (GELU_K1 + GELU_K3 * (a * a))))
        ga_ref[rows, :] = (gl.astype(BF16).reshape(sub, BF16_ROWS, tokens) * gate
                           ).reshape(PEER_KEYS, tokens)
    acc_ref[...] += _dot(vt_ref[...], ga_ref[...])

    @pl.when(j == pl.num_programs(1) - 1)
    def _():
        y = x_ref[...] + acc_ref[...].T
        if final_norm:
            y = _rmsnorm(y, fgain_ref[...])
        o_ref[...] = y


def _peer_flat(x2, ht, rank2, e2, cnt, e1, u16, vt16, fgain, tile, etile, final_norm):
    n, d = x2.shape
    n_exp = u16.shape[0]
    rows = etile // PEER_KEYS
    packed = lambda: pl.BlockSpec((PEER_HEADS, PEER_KEYS // BF16_ROWS, BF16_ROWS, tile),
                                  lambda i, j: (0, 0, 0, i))
    row_words = lambda: pl.BlockSpec((PEER_HEADS, rows, tile), lambda i, j: (0, j, i))
    split_rows = lambda a: a.reshape(PEER_HEADS, PEER_KEYS // BF16_ROWS, BF16_ROWS, n)
    return pl.pallas_call(
        functools.partial(_peer_flat_kernel, final_norm=final_norm),
        grid=(n // tile, n_exp // etile),
        in_specs=[pl.BlockSpec((tile, d), lambda i, j: (i, 0)),
                  pl.BlockSpec((d, tile), lambda i, j: (0, i)),
                  packed(), packed(), row_words(), row_words(),
                  pl.BlockSpec((etile, d), lambda i, j: (j, 0)),
                  pl.BlockSpec((d, etile), lambda i, j: (0, j)),
                  pl.BlockSpec(fgain.shape, lambda i, j: (0, 0))],
        out_specs=pl.BlockSpec((tile, d), lambda i, j: (i, 0)),
        out_shape=jax.ShapeDtypeStruct((n, d), F32),
        scratch_shapes=[pltpu.VMEM((d, tile), F32), pltpu.VMEM((etile, tile), F32),
                        pltpu.VMEM((etile, tile), BF16)],
        compiler_params=_params("parallel", "arbitrary"),
        name="peer_flat",
    )(x2, ht, split_rows(rank2), split_rows(e2), cnt, e1, u16, vt16, fgain)


def _tile(n, want):
    t = min(want, n)
    while n % t:
        t //= 2
    return t


def kernel(x, mem, norm_mix, w_in, sg_v_gain, sg_w_spatial, sg_b_spatial, gla_w_gate, gla_b_gate,
           gla_out_gain, w_out, norm_mem, mem_gain, w_cq, w_ckv, w_co, norm_ffn, peer_w_q,
           peer_sub_keys, peer_u, peer_v, final_gain):
    b, s, d = x.shape
    n = b * s
    depth = w_in.shape[0]
    row_tile = _tile(s, 512)
    sb_tile = _tile(s, 512)
    sel_tile = _tile(n, 256)
    main_tile = _tile(n, 512)
    etile = 1024

    sb_end = 3 * SB_WIDTH
    sg_end = sb_end + 2 * SG_WIDTH
    gl_end = sg_end + 2 * GLA_KEY_WIDTH + 2 * GLA_WIDTH
    x2 = x.reshape(n, d)
    for l in range(depth):
        w16 = w_in[l].astype(BF16)
        wga = jnp.pad(w16[:, gl_end:], ((0, 0), (0, LANES - GLA_GATE_RANK)))
        wgate = jnp.pad(gla_w_gate[l].astype(BF16), ((0, LANES - GLA_GATE_RANK), (0, 0)))
        sb, sg, gl, la = _inproj(x2, norm_mix[l][None], w16[:, :sb_end], w16[:, sb_end:sg_end],
                                 w16[:, sg_end:gl_end], wga, wgate, gla_b_gate[l][None], row_tile)
        ysb = _sb_attn(sb.reshape(b, s, -1), sb_tile if l == 0 else _tile(s, 2 * sb_tile), split=False)
        bias = jnp.repeat(sg_b_spatial[l].T, SG_GROUP_DIM, axis=1)
        yrest = _sgu_gla(sg.reshape(b, s, -1), gl.reshape(b, s, -1), la.reshape(b, s, -1),
                         sg_v_gain[l][None], sg_w_spatial[l], bias, gla_out_gain[l][None])
        x2 = _outproj(x2, ysb.reshape(n, -1), yrest.reshape(n, -1), w_out[l].astype(BF16), row_tile)

        kv = _mem_kv(mem, mem_gain[l][None], w_ckv[l].astype(BF16))
        x2 = _cross(x2, norm_mem[l][None], w_cq[l].astype(BF16), kv, w_co[l].astype(BF16),
                    row_tile, s)

        keys = peer_sub_keys[l].astype(BF16)
        ht, rank2, e2, cnt, e1 = _peer_select(x2, norm_ffn[l][None], peer_w_q[l].T.astype(BF16),
                                              keys, sel_tile, looped=False)
        vt = peer_v[l].T.astype(BF16)
        x2 = _peer_flat(x2, ht, rank2, e2, cnt, e1, peer_u[l].astype(BF16),
                        vt, final_gain[None], main_tile, etile if l == 0 else 2 * etile,
                        final_norm=(l == depth - 1))
    return x2.reshape(b, s, d)
```

```python
import functools

import jax
import jax.numpy as jnp
import numpy as np
from jax import lax
from jax.experimental import pallas as pl
from jax.experimental.pallas import tpu as pltpu

EPS = 1e-6
LOG2E = 1.4426950408889634

LANES = 128
BF16_ROWS = 16
GELU_K1 = -2.0 * 0.7978845608028654 * LOG2E
GELU_K3 = GELU_K1 * 0.044715
VMEM_LIMIT_BYTES = 56 * 1024 * 1024

SB_HEADS = 8
SB_HEAD_DIM = 64
SB_WIDTH = SB_HEADS * SB_HEAD_DIM
SG_GROUPS = 4
SG_GROUP_DIM = 64
SG_WIDTH = SG_GROUPS * SG_GROUP_DIM
GLA_HEADS = 4
GLA_DK = 32
GLA_DV = 64
GLA_KEY_WIDTH = GLA_HEADS * GLA_DK
GLA_WIDTH = GLA_HEADS * GLA_DV
GLA_GATE_RANK = 16
GLA_TAU = 16.0
CHUNK = 128
X_HEADS = 4
PEER_HEADS = 8
PEER_KEYS = 128
PEER_TOPK = 16
PEER_HALF = 64
SB_UNROLL = 4

BF16 = jnp.bfloat16
F32 = jnp.float32


def _params(*semantics):
    return pltpu.CompilerParams(dimension_semantics=semantics,
                                vmem_limit_bytes=VMEM_LIMIT_BYTES)


def _dot(a, b):
    return jnp.dot(a, b, preferred_element_type=F32)


def _dot_nt(a, b):
    return lax.dot_general(a, b, (((1,), (1,)), ((), ())), preferred_element_type=F32)


def _split_dot(a, b16):
    hi = a.astype(BF16)
    lo = (a - hi.astype(F32)).astype(BF16)
    return _dot(hi, b16) + _dot(lo, b16)


def _rmsnorm(x, gain):
    return x * lax.rsqrt(jnp.mean(x * x, axis=-1, keepdims=True) + EPS) * gain


def _log_sigmoid(x):
    return jnp.minimum(x, 0.0) - jnp.log1p(jnp.exp(-jnp.abs(x)))


def _full(shape):
    n = len(shape)
    return pl.BlockSpec(shape, lambda *_: (0,) * n)


def _inproj_kernel(x_ref, gain_ref, wsb_ref, wsg_ref, wgl_ref, wga_ref, wgate_ref, bgate_ref,
                   sb_ref, sg_ref, gl_ref, la_ref):
    h = _rmsnorm(x_ref[...], gain_ref[...]).astype(BF16)
    sb = _dot(h, wsb_ref[...])
    sb_ref[:, :SB_WIDTH] = (sb[:, :SB_WIDTH] * (SB_HEAD_DIM ** -0.5 * LOG2E)).astype(BF16)
    sb_ref[:, SB_WIDTH:] = sb[:, SB_WIDTH:].astype(BF16)
    sg_ref[...] = _dot(h, wsg_ref[...])
    gl_ref[...] = _dot(h, wgl_ref[...])
    ga = _dot(h, wga_ref[...])
    gate = _dot(ga.astype(BF16), wgate_ref[...]) + bgate_ref[...]
    la_ref[...] = _log_sigmoid(gate) * (1.0 / GLA_TAU)


def _inproj(x2, gain, wsb, wsg, wgl, wga, wgate, bgate, tile):
    n, d = x2.shape
    row = lambda w: pl.BlockSpec((tile, w), lambda i: (i, 0))
    return pl.pallas_call(
        _inproj_kernel,
        grid=(n // tile,),
        in_specs=[row(d), _full(gain.shape), _full(wsb.shape), _full(wsg.shape), _full(wgl.shape),
                  _full(wga.shape), _full(wgate.shape), _full(bgate.shape)],
        out_specs=[row(3 * SB_WIDTH), row(2 * SG_WIDTH), row(2 * GLA_KEY_WIDTH + 2 * GLA_WIDTH),
                   row(GLA_KEY_WIDTH)],
        out_shape=[jax.ShapeDtypeStruct((n, 3 * SB_WIDTH), BF16),
                   jax.ShapeDtypeStruct((n, 2 * SG_WIDTH), F32),
                   jax.ShapeDtypeStruct((n, 2 * GLA_KEY_WIDTH + 2 * GLA_WIDTH), F32),
                   jax.ShapeDtypeStruct((n, GLA_KEY_WIDTH), F32)],
        compiler_params=_params("parallel"),
        name="inproj",
    )(x2, gain, wsb, wsg, wgl, wga, wgate, bgate)


def _sb_kernel(q_ref, k_ref, v_ref, o_ref, acc_ref, carry_ref, *, tq):
    qi = pl.program_id(2)
    nsub = tq // CHUNK
    lane = lax.broadcasted_iota(jnp.int32, (CHUNK, LANES), 1)
    head0 = lane < SB_HEAD_DIM
    r_i = lax.broadcasted_iota(jnp.int32, (2 * LANES, 2 * LANES), 0)
    c_i = lax.broadcasted_iota(jnp.int32, (2 * LANES, 2 * LANES), 1)
    m_incl = jnp.where((r_i >= c_i) & ((r_i >= LANES) == (c_i >= LANES)), -1.0, 0.0).astype(BF16)

    acc_ref[...] = jnp.zeros_like(acc_ref)
    carry_ref[...] = jnp.zeros_like(carry_ref)

    def blocks(kjs, row0, masked):
        rows = slice(row0, tq)
        nrow = tq - row0
        q = q_ref[0, rows, :]
        carry = carry_ref[rows, :]
        acc = acc_ref[rows, :]
        for kj in kjs:
            start = pl.multiple_of(kj * CHUNK, CHUNK)
            kb = k_ref[0, pl.ds(start, CHUNK), :]
            vb = v_ref[0, pl.ds(start, CHUNK), :]
            zero = jnp.zeros_like(kb)
            k2 = jnp.concatenate([jnp.where(head0, kb, zero), jnp.where(head0, zero, kb)], axis=0)
            v2 = jnp.concatenate([jnp.where(head0, vb, zero), jnp.where(head0, zero, vb)], axis=0)
            z = _dot_nt(q, k2)
            nl = jnp.maximum(z, 0.0) + jnp.log(1.0 + jnp.exp2(-jnp.abs(z))) * LOG2E
            if masked:
                t_loc = row0 + lax.broadcasted_iota(jnp.int32, (nrow, 2 * LANES), 0)
                s_loc = row0 + (lax.broadcasted_iota(jnp.int32, (nrow, 2 * LANES), 1) & (LANES - 1))
                mask = s_loc < t_loc
                nl = jnp.where(mask, nl, 0.0)
            incl = _dot(nl.astype(BF16), m_incl)
            w = jnp.exp2(z + incl + carry)
            if masked:
                w = jnp.where(mask, w, 0.0)
            acc = acc + _dot(w.astype(BF16), v2)
            carry = carry + jnp.concatenate(
                [jnp.broadcast_to(incl[:, 0:1], (nrow, LANES)),
                 jnp.broadcast_to(incl[:, LANES:LANES + 1], (nrow, LANES))], axis=1)
        carry_ref[rows, :] = carry
        acc_ref[rows, :] = acc

    for c in reversed(range(nsub)):
        blocks([qi * nsub + c], c * CHUNK, True)

    def body(it, _):
        first = qi * nsub - 1 - it * SB_UNROLL
        blocks([first - u for u in range(SB_UNROLL)], 0, False)
        return 0

    lax.fori_loop(0, qi * (nsub // SB_UNROLL), body, 0)
    o_ref[0] = acc_ref[...].astype(BF16)


def _sb_attn(sb, tq):
    b, s, _ = sb.shape
    pairs = SB_WIDTH // LANES
    return pl.pallas_call(
        functools.partial(_sb_kernel, tq=tq),
        grid=(b, pairs, s // tq),
        in_specs=[pl.BlockSpec((1, tq, LANES), lambda bi, hp, qi: (bi, qi, hp)),
                  pl.BlockSpec((1, s, LANES), lambda bi, hp, qi: (bi, 0, pairs + hp)),
                  pl.BlockSpec((1, s, LANES), lambda bi, hp, qi: (bi, 0, 2 * pairs + hp))],
        out_specs=pl.BlockSpec((1, tq, LANES), lambda bi, hp, qi: (bi, qi, hp)),
        out_shape=jax.ShapeDtypeStruct((b, s, SB_WIDTH), BF16),
        scratch_shapes=[pltpu.VMEM((tq, LANES), F32), pltpu.VMEM((tq, 2 * LANES), F32)],
        compiler_params=_params("parallel", "parallel", "parallel"),
        name="sb_attn",
    )(sb, sb, sb)


GLA_LEVELS = 7


def _gla_level_tables():
    r = np.arange(CHUNK)
    sel = np.zeros((GLA_LEVELS, CHUNK, CHUNK), np.float32)
    upper = np.zeros((GLA_LEVELS, CHUNK, GLA_KEY_WIDTH), np.float32)
    pair = np.zeros((GLA_LEVELS + 1, CHUNK, GLA_HEADS, CHUNK), np.float32)
    for lv in range(GLA_LEVELS):
        half = CHUNK >> (lv + 1)
        blk = r // half
        is_upper = blk % 2 == 1
        ref = np.where(is_upper, blk * half - 1, (blk + 1) * half - 1)
        sel[lv, r, ref] = 1.0
        upper[lv, is_upper, :] = 1.0
        pair[lv] = ((r[:, None] // (2 * half)) == (r[None, :] // (2 * half)))[:, None, :]
    pair[GLA_LEVELS] = (r[:, None] == r[None, :])[:, None, :]
    return (jnp.asarray(sel.reshape(GLA_LEVELS * CHUNK, CHUNK), BF16), jnp.asarray(upper),
            jnp.asarray(pair.reshape(GLA_LEVELS + 1, CHUNK, GLA_HEADS * CHUNK)))


def _sgu_gla_kernel(sg_ref, gl_ref, la_ref, sgain_ref, ws_ref, bs_ref, ogain_ref, sel_ref, upper_ref,
                    pair_ref, y_ref, state_ref):
    @pl.when(pl.program_id(1) == 0)
    def _():
        state_ref[...] = jnp.zeros_like(state_ref)

    row = lax.broadcasted_iota(jnp.int32, (CHUNK, CHUNK), 0)
    col = lax.broadcasted_iota(jnp.int32, (CHUNK, CHUNK), 1)
    causal = row >= col
    lane_w = lax.broadcasted_iota(jnp.int32, (CHUNK, SG_WIDTH), 1)

    u = jax.nn.gelu(sg_ref[0, :, :SG_WIDTH])
    v = _rmsnorm(jax.nn.gelu(sg_ref[0, :, SG_WIDTH:]), sgain_ref[...]).astype(BF16)
    mixed = bs_ref[...]
    for g in range(SG_GROUPS):
        wg = jnp.where(causal, ws_ref[g], 0.0).astype(BF16)
        in_group = (lane_w >= g * SG_GROUP_DIM) & (lane_w < (g + 1) * SG_GROUP_DIM)
        mixed = mixed + _dot(wg, jnp.where(in_group, v, jnp.zeros_like(v)))
    y_ref[0, :, :SG_WIDTH] = (u * mixed).astype(BF16)

    kw = GLA_KEY_WIDTH
    q = gl_ref[0, :, :kw] * GLA_DK ** -0.5
    k = gl_ref[0, :, kw:2 * kw]
    vv = gl_ref[0, :, 2 * kw:2 * kw + GLA_WIDTH]
    og = gl_ref[0, :, 2 * kw + GLA_WIDTH:]
    tri = jnp.where(causal, 1.0, 0.0).astype(BF16)
    la = la_ref[0]
    la_hi = la.astype(BF16)
    la_lo = (la - la_hi.astype(F32)).astype(BF16)
    b = _dot(tri, la_hi) + _dot(tri, la_lo)
    b_t = b.T
    k_t = k.T
    b_end_col = b_t[:, CHUNK - 1:CHUNK]

    lane_v = lax.broadcasted_iota(jnp.int32, (CHUNK, GLA_WIDTH), 1)
    lane_k = lax.broadcasted_iota(jnp.int32, (CHUNK, kw), 1)
    state = state_ref[...]
    o = _dot((q * jnp.exp(b)).astype(BF16), state.astype(BF16))
    vv16 = vv.astype(BF16)

    b_hi = b.astype(BF16)
    b_mid = (b - b_hi.astype(F32)).astype(BF16)
    b_lo = (b - b_hi.astype(F32) - b_mid.astype(F32)).astype(BF16)
    b_ref = _dot(sel_ref[...], b_hi) + _dot(sel_ref[...], b_mid) + _dot(sel_ref[...], b_lo)
    key_heads = [(lane_k >= h * GLA_DK) & (lane_k < (h + 1) * GLA_DK) for h in range(GLA_HEADS)]
    scores = jnp.zeros((CHUNK, GLA_HEADS * CHUNK), F32)
    for lv in range(GLA_LEVELS + 1):
        if lv < GLA_LEVELS:
            f = jnp.exp(-jnp.abs(b - b_ref[lv * CHUNK:(lv + 1) * CHUNK, :]))
            ql = (q * f * upper_ref[lv]).astype(BF16)
            kl = (k * f * (1.0 - upper_ref[lv])).astype(BF16)
        else:
            ql, kl = q.astype(BF16), k.astype(BF16)
        kl4 = jnp.concatenate([jnp.where(m, kl, jnp.zeros_like(kl)) for m in key_heads], axis=0)
        scores = scores + _dot_nt(ql, kl4) * pair_ref[lv]
    v4 = jnp.concatenate(
        [jnp.where((lane_v >= h * GLA_DV) & (lane_v < (h + 1) * GLA_DV), vv16, jnp.zeros_like(vv16))
         for h in range(GLA_HEADS)], axis=0)
    o = o + _dot(scores.astype(BF16), v4)

    kd_t = (k_t * jnp.exp(b_end_col - b_t)).astype(BF16)
    row_h = lax.broadcasted_iota(jnp.int32, (kw, GLA_WIDTH), 0) // GLA_DK
    col_h = lax.broadcasted_iota(jnp.int32, (kw, GLA_WIDTH), 1) // GLA_DV
    state_ref[...] = jnp.exp(b_end_col) * state + jnp.where(row_h == col_h, _dot(kd_t, vv16), 0.0)

    r2 = lax.broadcasted_iota(jnp.int32, (GLA_WIDTH, GLA_WIDTH), 0) // GLA_DV
    c2 = lax.broadcasted_iota(jnp.int32, (GLA_WIDTH, GLA_WIDTH), 1) // GLA_DV
    avg = jnp.where(r2 == c2, 1.0 / GLA_DV, 0.0).astype(BF16)
    msq = _split_dot(o * o, avg)
    y = o * lax.rsqrt(msq + EPS) * ogain_ref[...]
    y_ref[0, :, SG_WIDTH:] = (y * (og * (1.0 / (1.0 + jnp.exp(-og))))).astype(BF16)


def _sgu_gla(sg, gl, la, sgain, ws, bs, ogain):
    b, s, _ = sg.shape
    blk = lambda w: pl.BlockSpec((1, CHUNK, w), lambda bi, ci: (bi, ci, 0))
    sel, upper, pair = _gla_level_tables()
    return pl.pallas_call(
        _sgu_gla_kernel,
        grid=(b, s // CHUNK),
        in_specs=[blk(sg.shape[2]), blk(gl.shape[2]), blk(la.shape[2]),
                  _full(sgain.shape), _full(ws.shape), _full(bs.shape), _full(ogain.shape),
                  _full(sel.shape), _full(upper.shape), _full(pair.shape)],
        out_specs=blk(SG_WIDTH + GLA_WIDTH),
        out_shape=jax.ShapeDtypeStruct((b, s, SG_WIDTH + GLA_WIDTH), BF16),
        scratch_shapes=[pltpu.VMEM((GLA_KEY_WIDTH, GLA_WIDTH), F32)],
        compiler_params=_params("parallel", "arbitrary"),
        name="sgu_gla",
    )(sg, gl, la, sgain, ws, bs, ogain, sel, upper, pair)


def _outproj_kernel(x_ref, ysb_ref, yrest_ref, w_ref, o_ref):
    o_ref[...] = (x_ref[...] + _dot(ysb_ref[...], w_ref[:SB_WIDTH, :])
                  + _dot(yrest_ref[...], w_ref[SB_WIDTH:, :]))


def _outproj(x2, ysb, yrest, w, tile):
    n, d = x2.shape
    row = lambda w_: pl.BlockSpec((tile, w_), lambda i: (i, 0))
    return pl.pallas_call(
        _outproj_kernel,
        grid=(n // tile,),
        in_specs=[row(d), row(ysb.shape[1]), row(yrest.shape[1]), _full(w.shape)],
        out_specs=row(d),
        out_shape=jax.ShapeDtypeStruct((n, d), F32),
        compiler_params=_params("parallel"),
        name="outproj",
    )(x2, ysb, yrest, w)


def _mem_kv_kernel(mem_ref, gain_ref, w_ref, kv_ref):
    m = _rmsnorm(mem_ref[0], gain_ref[...]).astype(BF16)
    kv_ref[0] = _dot(m, w_ref[...]).astype(BF16)


def _mem_kv(mem, gain, w):
    b, m, d = mem.shape
    return pl.pallas_call(
        _mem_kv_kernel,
        grid=(b,),
        in_specs=[pl.BlockSpec((1, m, d), lambda bi: (bi, 0, 0)), _full(gain.shape), _full(w.shape)],
        out_specs=pl.BlockSpec((1, m, 2 * d), lambda bi: (bi, 0, 0)),
        out_shape=jax.ShapeDtypeStruct((b, m, 2 * d), BF16),
        compiler_params=_params("parallel"),
        name="mem_kv",
    )(mem, gain, w)


def _cross_kernel(x_ref, gain_ref, wq_ref, kv_ref, wo_ref, o_ref, att_ref):
    x = x_ref[...]
    d = x.shape[1]
    hd = d // X_HEADS
    h = _rmsnorm(x, gain_ref[...]).astype(BF16)
    q = (_dot(h, wq_ref[...]) * hd ** -0.5).astype(BF16)
    for a in range(X_HEADS):
        kh = kv_ref[0, :, a * hd:(a + 1) * hd]
        vh = kv_ref[0, :, d + a * hd:d + (a + 1) * hd]
        s = _dot_nt(q[:, a * hd:(a + 1) * hd], kh)
        e = jnp.exp(s - jnp.max(s, axis=-1, keepdims=True))
        p = e / jnp.sum(e, axis=-1, keepdims=True)
        att_ref[:, a * hd:(a + 1) * hd] = _dot(p.astype(BF16), vh).astype(BF16)
    o_ref[...] = x + _dot(att_ref[...], wo_ref[...])


def _cross(x2, gain, wq, kv, wo, tile, seq):
    n, d = x2.shape
    tiles_per_seq = seq // tile
    row = pl.BlockSpec((tile, d), lambda i: (i, 0))
    return pl.pallas_call(
        _cross_kernel,
        grid=(n // tile,),
        in_specs=[row, _full(gain.shape), _full(wq.shape),
                  pl.BlockSpec((1,) + kv.shape[1:], lambda i: (i // tiles_per_seq, 0, 0)),
                  _full(wo.shape)],
        out_specs=row,
        out_shape=jax.ShapeDtypeStruct((n, d), F32),
        scratch_shapes=[pltpu.VMEM((tile, d), BF16)],
        compiler_params=_params("parallel"),
        name="cross",
    )(x2, gain, wq, kv, wo)


_CAND_PAIRS = [(r1, r2) for r1 in range(PEER_TOPK) for r2 in range(PEER_TOPK)
               if (r1 + 1) * (r2 + 1) <= PEER_TOPK]


def _top16(s):
    work = s
    rank = jnp.full(s.shape, float(PEER_KEYS - 1), F32)
    vals = []
    for r in range(PEER_TOPK):
        m = jnp.max(work, axis=0, keepdims=True)
        hit = work == m
        rank = jnp.where(hit, float(r), rank)
        work = jnp.where(hit, -jnp.inf, work)
        vals.append(m)
    return vals, rank


def _bf16_pair_words(x):
    u = lax.bitcast_convert_type(x.astype(BF16).astype(F32), jnp.uint32)
    return u | (u >> 16)


def _peer_select_kernel(x_ref, gain_ref, wqt_ref, keys_ref, ht_ref, rank2_ref, e2_ref, cnt_ref, e1_ref,
                        qt_ref):
    h_t = _rmsnorm(x_ref[...], gain_ref[...]).T.astype(BF16)
    ht_ref[...] = h_t
    qt_ref[...] = _dot(wqt_ref[...], h_t).astype(BF16)
    tokens = h_t.shape[1]

    def head(hd):
        base = hd * 2 * PEER_HALF
        s1 = _dot(keys_ref[hd, 0], qt_ref[pl.ds(base, PEER_HALF), :])
        s2 = _dot(keys_ref[hd, 1], qt_ref[pl.ds(base + PEER_HALF, PEER_HALF), :])
        v1, _ = _top16(s1)
        v2, rank2 = _top16(s2)
        cands = [v1[r1] + v2[r2] for r1, r2 in _CAND_PAIRS]
        pad = -len(cands) % 8
        cands = jnp.concatenate(cands + [jnp.full((pad, tokens), -jnp.inf, F32)], axis=0)
        top = []
        for _ in range(PEER_TOPK):
            m = jnp.max(cands, axis=0, keepdims=True)
            cands = jnp.where(cands == m, -jnp.inf, cands)
            top.append(m)
        tau = top[-1]
        top = jnp.concatenate(top, axis=0)
        z = jnp.sum(jnp.exp(top - top[0:1, :]), axis=0, keepdims=True)
        cnt = jnp.zeros(s1.shape, F32)
        for r2 in range(PEER_TOPK):
            cnt = jnp.where(s1 + v2[r2] >= tau, float(r2 + 1), cnt)
        rank2_ref[hd] = rank2.astype(BF16)
        e2_ref[hd] = jnp.exp(s2 - v2[0]).astype(BF16)
        cnt_ref[hd] = _bf16_pair_words(cnt)
        e1_ref[hd] = _bf16_pair_words(jnp.exp(s1 - v1[0]) / z)

    for hd in range(PEER_HEADS):
        head(hd)


def _peer_select(x2, gain, wqt, keys, tile):
    n, d = x2.shape
    per_head = lambda: pl.BlockSpec((PEER_HEADS, PEER_KEYS, tile), lambda i: (0, 0, i))
    sel_shape = (PEER_HEADS, PEER_KEYS, n)
    return pl.pallas_call(
        _peer_select_kernel,
        grid=(n // tile,),
        in_specs=[pl.BlockSpec((tile, d), lambda i: (i, 0)), _full(gain.shape), _full(wqt.shape),
                  _full(keys.shape)],
        out_specs=[pl.BlockSpec((d, tile), lambda i: (0, i)),
                   per_head(), per_head(), per_head(), per_head()],
        out_shape=[jax.ShapeDtypeStruct((d, n), BF16),
                   jax.ShapeDtypeStruct(sel_shape, BF16), jax.ShapeDtypeStruct(sel_shape, BF16),
                   jax.ShapeDtypeStruct(sel_shape, jnp.uint32),
                   jax.ShapeDtypeStruct(sel_shape, jnp.uint32)],
        scratch_shapes=[pltpu.VMEM((wqt.shape[0], tile), BF16)],
        compiler_params=_params("parallel"),
        name="peer_select",
    )(x2, gain, wqt, keys)


def _peer_flat_kernel(x_ref, ht_ref, rank2_ref, e2_ref, cnt_ref, e1_ref, u_ref, vt_ref, fgain_ref,
                      o_ref, acc_ref, act_ref, ga_ref, *, final_norm):
    j = pl.program_id(1)
    tokens = ht_ref.shape[1]
    sub = PEER_KEYS // BF16_ROWS

    @pl.when(j == 0)
    def _():
        acc_ref[...] = jnp.zeros_like(acc_ref)

    def pair_row(ref, hd, row):
        words = jnp.broadcast_to(ref[hd, row:row + 1, :], (BF16_ROWS // 2, tokens))
        return pltpu.bitcast(words, BF16)

    act_ref[...] = _dot(u_ref[...], ht_ref[...])
    for ii in range(u_ref.shape[0] // PEER_KEYS):
        gate = jnp.zeros((sub, BF16_ROWS, tokens), BF16)
        for hd in range(PEER_HEADS):
            w = e2_ref[hd] * pair_row(e1_ref, hd, ii)[None]
            hit = rank2_ref[hd] < pair_row(cnt_ref, hd, ii)[None]
            gate = gate + jnp.where(hit, w, jnp.zeros_like(w))
        rows = slice(ii * PEER_KEYS, (ii + 1) * PEER_KEYS)
        a = act_ref[rows, :]
        gl = a / (1.0 + jnp.exp2(a * (GELU_K1 + GELU_K3 * (a * a))))
        ga_ref[rows, :] = (gl.astype(BF16).reshape(sub, BF16_ROWS, tokens) * gate
                           ).reshape(PEER_KEYS, tokens)
    acc_ref[...] += _dot(vt_ref[...], ga_ref[...])

    @pl.when(j == pl.num_programs(1) - 1)
    def _():
        y = x_ref[...] + acc_ref[...].T
        if final_norm:
            y = _rmsnorm(y, fgain_ref[...])
        o_ref[...] = y


def _peer_flat(x2, ht, rank2, e2, cnt, e1, u16, vt16, fgain, tile, etile, final_norm):
    n, d = x2.shape
    n_exp = u16.shape[0]
    rows = etile // PEER_KEYS
    packed = lambda: pl.BlockSpec((PEER_HEADS, PEER_KEYS // BF16_ROWS, BF16_ROWS, tile),
                                  lambda i, j: (0, 0, 0, i))
    row_words = lambda: pl.BlockSpec((PEER_HEADS, rows, tile), lambda i, j: (0, j, i))
    split_rows = lambda a: a.reshape(PEER_HEADS, PEER_KEYS // BF16_ROWS, BF16_ROWS, n)
    return pl.pallas_call(
        functools.partial(_peer_flat_kernel, final_norm=final_norm),
        grid=(n // tile, n_exp // etile),
        in_specs=[pl.BlockSpec((tile, d), lambda i, j: (i, 0)),
                  pl.BlockSpec((d, tile), lambda i, j: (0, i)),
                  packed(), packed(), row_words(), row_words(),
                  pl.BlockSpec((etile, d), lambda i, j: (j, 0)),
                  pl.BlockSpec((d, etile), lambda i, j: (0, j)),
                  pl.BlockSpec(fgain.shape, lambda i, j: (0, 0))],
        out_specs=pl.BlockSpec((tile, d), lambda i, j: (i, 0)),
        out_shape=jax.ShapeDtypeStruct((n, d), F32),
        scratch_shapes=[pltpu.VMEM((d, tile), F32), pltpu.VMEM((etile, tile), F32),
                        pltpu.VMEM((etile, tile), BF16)],
        compiler_params=_params("parallel", "arbitrary"),
        name="peer_flat",
    )(x2, ht, split_rows(rank2), split_rows(e2), cnt, e1, u16, vt16, fgain)


def _tile(n, want):
    t = min(want, n)
    while n % t:
        t //= 2
    return t


def kernel(x, mem, norm_mix, w_in, sg_v_gain, sg_w_spatial, sg_b_spatial, gla_w_gate, gla_b_gate,
           gla_out_gain, w_out, norm_mem, mem_gain, w_cq, w_ckv, w_co, norm_ffn, peer_w_q,
           peer_sub_keys, peer_u, peer_v, final_gain):
    b, s, d = x.shape
    n = b * s
    depth = w_in.shape[0]
    row_tile = _tile(s, 512)
    sb_tile = _tile(s, 1024)
    sel_tile = _tile(n, 256)
    main_tile = _tile(n, 512)
    etile = 2048

    sb_end = 3 * SB_WIDTH
    sg_end = sb_end + 2 * SG_WIDTH
    gl_end = sg_end + 2 * GLA_KEY_WIDTH + 2 * GLA_WIDTH
    x2 = x.reshape(n, d)
    for l in range(depth):
        w16 = w_in[l].astype(BF16)
        wga = jnp.pad(w16[:, gl_end:], ((0, 0), (0, LANES - GLA_GATE_RANK)))
        wgate = jnp.pad(gla_w_gate[l].astype(BF16), ((0, LANES - GLA_GATE_RANK), (0, 0)))
        sb, sg, gl, la = _inproj(x2, norm_mix[l][None], w16[:, :sb_end], w16[:, sb_end:sg_end],
                                 w16[:, sg_end:gl_end], wga, wgate, gla_b_gate[l][None], row_tile)
        ysb = _sb_attn(sb.reshape(b, s, -1), sb_tile if l == 0 else _tile(s, 2 * sb_tile))
        bias = jnp.repeat(sg_b_spatial[l].T, SG_GROUP_DIM, axis=1)
        yrest = _sgu_gla(sg.reshape(b, s, -1), gl.reshape(b, s, -1), la.reshape(b, s, -1),
                         sg_v_gain[l][None], sg_w_spatial[l], bias, gla_out_gain[l][None])
        x2 = _outproj(x2, ysb.reshape(n, -1), yrest.reshape(n, -1), w_out[l].astype(BF16), row_tile)

        kv = _mem_kv(mem, mem_gain[l][None], w_ckv[l].astype(BF16))
        x2 = _cross(x2, norm_mem[l][None], w_cq[l].astype(BF16), kv, w_co[l].astype(BF16),
                    row_tile, s)

        keys = peer_sub_keys[l].astype(BF16)
        ht, rank2, e2, cnt, e1 = _peer_select(x2, norm_ffn[l][None], peer_w_q[l].T.astype(BF16),
                                              keys, sel_tile if l == 0 else sel_tile // 2)
        vt = peer_v[l].T.astype(BF16)
        x2 = _peer_flat(x2, ht, rank2, e2, cnt, e1, peer_u[l].astype(BF16),
                        vt, final_gain[None], main_tile if l == 0 else _tile(n, 2 * main_tile),
                        etile if l == 0 else etile // 2,
                        final_norm=(l == depth - 1))
    return x2.reshape(b, s, d)
```

```python
import functools

import jax
import jax.numpy as jnp
import numpy as np
from jax import lax
from jax.experimental import pallas as pl
from jax.experimental.pallas import tpu as pltpu

EPS = 1e-6
LOG2E = 1.4426950408889634

LANES = 128
BF16_ROWS = 16
GELU_K1 = -2.0 * 0.7978845608028654 * LOG2E
GELU_K3 = GELU_K1 * 0.044715
VMEM_LIMIT_BYTES = 56 * 1024 * 1024

SB_HEADS = 8
SB_HEAD_DIM = 64
SB_WIDTH = SB_HEADS * SB_HEAD_DIM
SG_GROUPS = 4
SG_GROUP_DIM = 64
SG_WIDTH = SG_GROUPS * SG_GROUP_DIM
GLA_HEADS = 4
GLA_DK = 32
GLA_DV = 64
GLA_KEY_WIDTH = GLA_HEADS * GLA_DK
GLA_WIDTH = GLA_HEADS * GLA_DV
GLA_GATE_RANK = 16
GLA_TAU = 16.0
CHUNK = 128
X_HEADS = 4
PEER_HEADS = 8
PEER_KEYS = 128
PEER_TOPK = 16
PEER_HALF = 64
SB_UNROLL = 4

BF16 = jnp.bfloat16
F32 = jnp.float32


def _params(*semantics):
    return pltpu.CompilerParams(dimension_semantics=semantics,
                                vmem_limit_bytes=VMEM_LIMIT_BYTES)


def _dot(a, b):
    return jnp.dot(a, b, preferred_element_type=F32)


def _dot_nt(a, b):
    return lax.dot_general(a, b, (((1,), (1,)), ((), ())), preferred_element_type=F32)


def _split_dot(a, b16):
    hi = a.astype(BF16)
    lo = (a - hi.astype(F32)).astype(BF16)
    return _dot(hi, b16) + _dot(lo, b16)


def _rmsnorm(x, gain):
    return x * lax.rsqrt(jnp.mean(x * x, axis=-1, keepdims=True) + EPS) * gain


def _log_sigmoid(x):
    return jnp.minimum(x, 0.0) - jnp.log1p(jnp.exp(-jnp.abs(x)))


def _full(shape):
    n = len(shape)
    return pl.BlockSpec(shape, lambda *_: (0,) * n)


def _inproj_kernel(x_ref, gain_ref, wsb_ref, wsg_ref, wgl_ref, wga_ref, wgate_ref, bgate_ref,
                   sb_ref, sg_ref, gl_ref, la_ref):
    h = _rmsnorm(x_ref[...], gain_ref[...]).astype(BF16)
    sb = _dot(h, wsb_ref[...])
    sb_ref[:, :SB_WIDTH] = (sb[:, :SB_WIDTH] * (SB_HEAD_DIM ** -0.5 * LOG2E)).astype(BF16)
    sb_ref[:, SB_WIDTH:] = sb[:, SB_WIDTH:].astype(BF16)
    sg_ref[...] = _dot(h, wsg_ref[...])
    gl_ref[...] = _dot(h, wgl_ref[...])
    ga = _dot(h, wga_ref[...])
    gate = _dot(ga.astype(BF16), wgate_ref[...]) + bgate_ref[...]
    la_ref[...] = _log_sigmoid(gate) * (1.0 / GLA_TAU)


def _inproj(x2, gain, wsb, wsg, wgl, wga, wgate, bgate, tile):
    n, d = x2.shape
    row = lambda w: pl.BlockSpec((tile, w), lambda i: (i, 0))
    return pl.pallas_call(
        _inproj_kernel,
        grid=(n // tile,),
        in_specs=[row(d), _full(gain.shape), _full(wsb.shape), _full(wsg.shape), _full(wgl.shape),
                  _full(wga.shape), _full(wgate.shape), _full(bgate.shape)],
        out_specs=[row(3 * SB_WIDTH), row(2 * SG_WIDTH), row(2 * GLA_KEY_WIDTH + 2 * GLA_WIDTH),
                   row(GLA_KEY_WIDTH)],
        out_shape=[jax.ShapeDtypeStruct((n, 3 * SB_WIDTH), BF16),
                   jax.ShapeDtypeStruct((n, 2 * SG_WIDTH), F32),
                   jax.ShapeDtypeStruct((n, 2 * GLA_KEY_WIDTH + 2 * GLA_WIDTH), F32),
                   jax.ShapeDtypeStruct((n, GLA_KEY_WIDTH), F32)],
        compiler_params=_params("parallel"),
        name="inproj",
    )(x2, gain, wsb, wsg, wgl, wga, wgate, bgate)


def _sb_kernel(q_ref, k_ref, v_ref, o_ref, acc_ref, carry_ref, *, tq):
    qi = pl.program_id(2)
    nsub = tq // CHUNK
    lane = lax.broadcasted_iota(jnp.int32, (CHUNK, LANES), 1)
    head0 = lane < SB_HEAD_DIM
    r_i = lax.broadcasted_iota(jnp.int32, (2 * LANES, 2 * LANES), 0)
    c_i = lax.broadcasted_iota(jnp.int32, (2 * LANES, 2 * LANES), 1)
    m_incl = jnp.where((r_i >= c_i) & ((r_i >= LANES) == (c_i >= LANES)), -1.0, 0.0).astype(BF16)

    acc_ref[...] = jnp.zeros_like(acc_ref)
    carry_ref[...] = jnp.zeros_like(carry_ref)

    def blocks(kjs, row0, masked):
        rows = slice(row0, tq)
        nrow = tq - row0
        q = q_ref[0, rows, :]
        carry = carry_ref[rows, :]
        acc = acc_ref[rows, :]
        for kj in kjs:
            start = pl.multiple_of(kj * CHUNK, CHUNK)
            kb = k_ref[0, pl.ds(start, CHUNK), :]
            vb = v_ref[0, pl.ds(start, CHUNK), :]
            zero = jnp.zeros_like(kb)
            k2 = jnp.concatenate([jnp.where(head0, kb, zero), jnp.where(head0, zero, kb)], axis=0)
            v2 = jnp.concatenate([jnp.where(head0, vb, zero), jnp.where(head0, zero, vb)], axis=0)
            z = _dot_nt(q, k2)
            nl = jnp.maximum(z, 0.0) + jnp.log(1.0 + jnp.exp2(-jnp.abs(z))) * LOG2E
            if masked:
                t_loc = row0 + lax.broadcasted_iota(jnp.int32, (nrow, 2 * LANES), 0)
                s_loc = row0 + (lax.broadcasted_iota(jnp.int32, (nrow, 2 * LANES), 1) & (LANES - 1))
                mask = s_loc < t_loc
                nl = jnp.where(mask, nl, 0.0)
            incl = _dot(nl.astype(BF16), m_incl)
            w = jnp.exp2(z + incl + carry)
            if masked:
                w = jnp.where(mask, w, 0.0)
            acc = acc + _dot(w.astype(BF16), v2)
            carry = carry + jnp.concatenate(
                [jnp.broadcast_to(incl[:, 0:1], (nrow, LANES)),
                 jnp.broadcast_to(incl[:, LANES:LANES + 1], (nrow, LANES))], axis=1)
        carry_ref[rows, :] = carry
        acc_ref[rows, :] = acc

    for c in reversed(range(nsub)):
        blocks([qi * nsub + c], c * CHUNK, True)

    def body(it, _):
        first = qi * nsub - 1 - it * SB_UNROLL
        blocks([first - u for u in range(SB_UNROLL)], 0, False)
        return 0

    lax.fori_loop(0, qi * (nsub // SB_UNROLL), body, 0)
    o_ref[0] = acc_ref[...].astype(BF16)


def _sb_attn(sb, tq):
    b, s, _ = sb.shape
    pairs = SB_WIDTH // LANES
    return pl.pallas_call(
        functools.partial(_sb_kernel, tq=tq),
        grid=(b, pairs, s // tq),
        in_specs=[pl.BlockSpec((1, tq, LANES), lambda bi, hp, qi: (bi, qi, hp)),
                  pl.BlockSpec((1, s, LANES), lambda bi, hp, qi: (bi, 0, pairs + hp)),
                  pl.BlockSpec((1, s, LANES), lambda bi, hp, qi: (bi, 0, 2 * pairs + hp))],
        out_specs=pl.BlockSpec((1, tq, LANES), lambda bi, hp, qi: (bi, qi, hp)),
        out_shape=jax.ShapeDtypeStruct((b, s, SB_WIDTH), BF16),
        scratch_shapes=[pltpu.VMEM((tq, LANES), F32), pltpu.VMEM((tq, 2 * LANES), F32)],
        compiler_params=_params("parallel", "parallel", "parallel"),
        name="sb_attn",
    )(sb, sb, sb)


GLA_LEVELS = 7


def _gla_level_tables():
    r = np.arange(CHUNK)
    sel = np.zeros((GLA_LEVELS, CHUNK, CHUNK), np.float32)
    upper = np.zeros((GLA_LEVELS, CHUNK, GLA_KEY_WIDTH), np.float32)
    pair = np.zeros((GLA_LEVELS + 1, CHUNK, GLA_HEADS, CHUNK), np.float32)
    for lv in range(GLA_LEVELS):
        half = CHUNK >> (lv + 1)
        blk = r // half
        is_upper = blk % 2 == 1
        ref = np.where(is_upper, blk * half - 1, (blk + 1) * half - 1)
        sel[lv, r, ref] = 1.0
        upper[lv, is_upper, :] = 1.0
        pair[lv] = ((r[:, None] // (2 * half)) == (r[None, :] // (2 * half)))[:, None, :]
    pair[GLA_LEVELS] = (r[:, None] == r[None, :])[:, None, :]
    return (jnp.asarray(sel.reshape(GLA_LEVELS * CHUNK, CHUNK), BF16), jnp.asarray(upper),
            jnp.asarray(pair.reshape(GLA_LEVELS + 1, CHUNK, GLA_HEADS * CHUNK)))


def _sgu_gla_kernel(sg_ref, gl_ref, la_ref, sgain_ref, ws_ref, bs_ref, ogain_ref, sel_ref, upper_ref,
                    pair_ref, y_ref, state_ref):
    @pl.when(pl.program_id(1) == 0)
    def _():
        state_ref[...] = jnp.zeros_like(state_ref)

    row = lax.broadcasted_iota(jnp.int32, (CHUNK, CHUNK), 0)
    col = lax.broadcasted_iota(jnp.int32, (CHUNK, CHUNK), 1)
    causal = row >= col
    lane_w = lax.broadcasted_iota(jnp.int32, (CHUNK, SG_WIDTH), 1)

    u = jax.nn.gelu(sg_ref[0, :, :SG_WIDTH])
    v = _rmsnorm(jax.nn.gelu(sg_ref[0, :, SG_WIDTH:]), sgain_ref[...]).astype(BF16)
    mixed = bs_ref[...]
    for g in range(SG_GROUPS):
        wg = jnp.where(causal, ws_ref[g], 0.0).astype(BF16)
        in_group = (lane_w >= g * SG_GROUP_DIM) & (lane_w < (g + 1) * SG_GROUP_DIM)
        mixed = mixed + _dot(wg, jnp.where(in_group, v, jnp.zeros_like(v)))
    y_ref[0, :, :SG_WIDTH] = (u * mixed).astype(BF16)

    kw = GLA_KEY_WIDTH
    q = gl_ref[0, :, :kw] * GLA_DK ** -0.5
    k = gl_ref[0, :, kw:2 * kw]
    vv = gl_ref[0, :, 2 * kw:2 * kw + GLA_WIDTH]
    og = gl_ref[0, :, 2 * kw + GLA_WIDTH:]
    tri = jnp.where(causal, 1.0, 0.0).astype(BF16)
    la = la_ref[0]
    la_hi = la.astype(BF16)
    la_lo = (la - la_hi.astype(F32)).astype(BF16)
    b = _dot(tri, la_hi) + _dot(tri, la_lo)
    b_t = b.T
    k_t = k.T
    b_end_col = b_t[:, CHUNK - 1:CHUNK]

    lane_v = lax.broadcasted_iota(jnp.int32, (CHUNK, GLA_WIDTH), 1)
    lane_k = lax.broadcasted_iota(jnp.int32, (CHUNK, kw), 1)
    state = state_ref[...]
    o = _dot((q * jnp.exp(b)).astype(BF16), state.astype(BF16))
    vv16 = vv.astype(BF16)

    b_hi = b.astype(BF16)
    b_mid = (b - b_hi.astype(F32)).astype(BF16)
    b_lo = (b - b_hi.astype(F32) - b_mid.astype(F32)).astype(BF16)
    b_ref = _dot(sel_ref[...], b_hi) + _dot(sel_ref[...], b_mid) + _dot(sel_ref[...], b_lo)
    key_heads = [(lane_k >= h * GLA_DK) & (lane_k < (h + 1) * GLA_DK) for h in range(GLA_HEADS)]
    scores = jnp.zeros((CHUNK, GLA_HEADS * CHUNK), F32)
    for lv in range(GLA_LEVELS + 1):
        if lv < GLA_LEVELS:
            f = jnp.exp(-jnp.abs(b - b_ref[lv * CHUNK:(lv + 1) * CHUNK, :]))
            ql = (q * f * upper_ref[lv]).astype(BF16)
            kl = (k * f * (1.0 - upper_ref[lv])).astype(BF16)
        else:
            ql, kl = q.astype(BF16), k.astype(BF16)
        kl4 = jnp.concatenate([jnp.where(m, kl, jnp.zeros_like(kl)) for m in key_heads], axis=0)
        scores = scores + _dot_nt(ql, kl4) * pair_ref[lv]
    v4 = jnp.concatenate(
        [jnp.where((lane_v >= h * GLA_DV) & (lane_v < (h + 1) * GLA_DV), vv16, jnp.zeros_like(vv16))
         for h in range(GLA_HEADS)], axis=0)
    o = o + _dot(scores.astype(BF16), v4)

    kd_t = (k_t * jnp.exp(b_end_col - b_t)).astype(BF16)
    row_h = lax.broadcasted_iota(jnp.int32, (kw, GLA_WIDTH), 0) // GLA_DK
    col_h = lax.broadcasted_iota(jnp.int32, (kw, GLA_WIDTH), 1) // GLA_DV
    state_ref[...] = jnp.exp(b_end_col) * state + jnp.where(row_h == col_h, _dot(kd_t, vv16), 0.0)

    r2 = lax.broadcasted_iota(jnp.int32, (GLA_WIDTH, GLA_WIDTH), 0) // GLA_DV
    c2 = lax.broadcasted_iota(jnp.int32, (GLA_WIDTH, GLA_WIDTH), 1) // GLA_DV
    avg = jnp.where(r2 == c2, 1.0 / GLA_DV, 0.0).astype(BF16)
    msq = _split_dot(o * o, avg)
    y = o * lax.rsqrt(msq + EPS) * ogain_ref[...]
    y_ref[0, :, SG_WIDTH:] = (y * (og * (1.0 / (1.0 + jnp.exp(-og))))).astype(BF16)


def _sgu_gla(sg, gl, la, sgain, ws, bs, ogain):
    b, s, _ = sg.shape
    blk = lambda w: pl.BlockSpec((1, CHUNK, w), lambda bi, ci: (bi, ci, 0))
    sel, upper, pair = _gla_level_tables()
    return pl.pallas_call(
        _sgu_gla_kernel,
        grid=(b, s // CHUNK),
        in_specs=[blk(sg.shape[2]), blk(gl.shape[2]), blk(la.shape[2]),
                  _full(sgain.shape), _full(ws.shape), _full(bs.shape), _full(ogain.shape),
                  _full(sel.shape), _full(upper.shape), _full(pair.shape)],
        out_specs=blk(SG_WIDTH + GLA_WIDTH),
        out_shape=jax.ShapeDtypeStruct((b, s, SG_WIDTH + GLA_WIDTH), BF16),
        scratch_shapes=[pltpu.VMEM((GLA_KEY_WIDTH, GLA_WIDTH), F32)],
        compiler_params=_params("parallel", "arbitrary"),
        name="sgu_gla",
    )(sg, gl, la, sgain, ws, bs, ogain, sel, upper, pair)


def _outproj_kernel(x_ref, ysb_ref, yrest_ref, w_ref, o_ref):
    o_ref[...] = (x_ref[...] + _dot(ysb_ref[...], w_ref[:SB_WIDTH, :])
                  + _dot(yrest_ref[...], w_ref[SB_WIDTH:, :]))


def _outproj(x2, ysb, yrest, w, tile):
    n, d = x2.shape
    row = lambda w_: pl.BlockSpec((tile, w_), lambda i: (i, 0))
    return pl.pallas_call(
        _outproj_kernel,
        grid=(n // tile,),
        in_specs=[row(d), row(ysb.shape[1]), row(yrest.shape[1]), _full(w.shape)],
        out_specs=row(d),
        out_shape=jax.ShapeDtypeStruct((n, d), F32),
        compiler_params=_params("parallel"),
        name="outproj",
    )(x2, ysb, yrest, w)


def _mem_kv_kernel(mem_ref, gain_ref, w_ref, kv_ref):
    m = _rmsnorm(mem_ref[0], gain_ref[...]).astype(BF16)
    kv_ref[0] = _dot(m, w_ref[...]).astype(BF16)


def _mem_kv(mem, gain, w):
    b, m, d = mem.shape
    return pl.pallas_call(
        _mem_kv_kernel,
        grid=(b,),
        in_specs=[pl.BlockSpec((1, m, d), lambda bi: (bi, 0, 0)), _full(gain.shape), _full(w.shape)],
        out_specs=pl.BlockSpec((1, m, 2 * d), lambda bi: (bi, 0, 0)),
        out_shape=jax.ShapeDtypeStruct((b, m, 2 * d), BF16),
        compiler_params=_params("parallel"),
        name="mem_kv",
    )(mem, gain, w)


def _cross_kernel(x_ref, gain_ref, wq_ref, kv_ref, wo_ref, o_ref, att_ref):
    x = x_ref[...]
    d = x.shape[1]
    hd = d // X_HEADS
    h = _rmsnorm(x, gain_ref[...]).astype(BF16)
    q = (_dot(h, wq_ref[...]) * hd ** -0.5).astype(BF16)
    for a in range(X_HEADS):
        kh = kv_ref[0, :, a * hd:(a + 1) * hd]
        vh = kv_ref[0, :, d + a * hd:d + (a + 1) * hd]
        s = _dot_nt(q[:, a * hd:(a + 1) * hd], kh)
        e = jnp.exp(s - jnp.max(s, axis=-1, keepdims=True))
        p = e / jnp.sum(e, axis=-1, keepdims=True)
        att_ref[:, a * hd:(a + 1) * hd] = _dot(p.astype(BF16), vh).astype(BF16)
    o_ref[...] = x + _dot(att_ref[...], wo_ref[...])


def _cross(x2, gain, wq, kv, wo, tile, seq):
    n, d = x2.shape
    tiles_per_seq = seq // tile
    row = pl.BlockSpec((tile, d), lambda i: (i, 0))
    return pl.pallas_call(
        _cross_kernel,
        grid=(n // tile,),
        in_specs=[row, _full(gain.shape), _full(wq.shape),
                  pl.BlockSpec((1,) + kv.shape[1:], lambda i: (i // tiles_per_seq, 0, 0)),
                  _full(wo.shape)],
        out_specs=row,
        out_shape=jax.ShapeDtypeStruct((n, d), F32),
        scratch_shapes=[pltpu.VMEM((tile, d), BF16)],
        compiler_params=_params("parallel"),
        name="cross",
    )(x2, gain, wq, kv, wo)


_CAND_PAIRS = [(r1, r2) for r1 in range(PEER_TOPK) for r2 in range(PEER_TOPK)
               if (r1 + 1) * (r2 + 1) <= PEER_TOPK]


def _top16(s):
    work = s
    rank = jnp.full(s.shape, float(PEER_KEYS - 1), F32)
    vals = []
    for r in range(PEER_TOPK):
        m = jnp.max(work, axis=0, keepdims=True)
        hit = work == m
        rank = jnp.where(hit, float(r), rank)
        work = jnp.where(hit, -jnp.inf, work)
        vals.append(m)
    return vals, rank


def _bf16_pair_words(x):
    u = lax.bitcast_convert_type(x.astype(BF16).astype(F32), jnp.uint32)
    return u | (u >> 16)


def _peer_select_kernel(x_ref, gain_ref, wqt_ref, keys_ref, ht_ref, rank2_ref, e2_ref, cnt_ref, e1_ref,
                        qt_ref):
    h_t = _rmsnorm(x_ref[...], gain_ref[...]).T.astype(BF16)
    ht_ref[...] = h_t
    qt_ref[...] = _dot(wqt_ref[...], h_t).astype(BF16)
    tokens = h_t.shape[1]

    def head(hd):
        base = hd * 2 * PEER_HALF
        s1 = _dot(keys_ref[hd, 0], qt_ref[pl.ds(base, PEER_HALF), :])
        s2 = _dot(keys_ref[hd, 1], qt_ref[pl.ds(base + PEER_HALF, PEER_HALF), :])
        v1, _ = _top16(s1)
        v2, rank2 = _top16(s2)
        cands = [v1[r1] + v2[r2] for r1, r2 in _CAND_PAIRS]
        pad = -len(cands) % 8
        cands = jnp.concatenate(cands + [jnp.full((pad, tokens), -jnp.inf, F32)], axis=0)
        top = []
        for _ in range(PEER_TOPK):
            m = jnp.max(cands, axis=0, keepdims=True)
            cands = jnp.where(cands == m, -jnp.inf, cands)
            top.append(m)
        tau = top[-1]
        top = jnp.concatenate(top, axis=0)
        z = jnp.sum(jnp.exp(top - top[0:1, :]), axis=0, keepdims=True)
        cnt = jnp.zeros(s1.shape, F32)
        for r2 in range(PEER_TOPK):
            cnt = jnp.where(s1 + v2[r2] >= tau, float(r2 + 1), cnt)
        rank2_ref[hd] = rank2.astype(BF16)
        e2_ref[hd] = jnp.exp(s2 - v2[0]).astype(BF16)
        cnt_ref[hd] = _bf16_pair_words(cnt)
        e1_ref[hd] = _bf16_pair_words(jnp.exp(s1 - v1[0]) / z)

    for hd in range(PEER_HEADS):
        head(hd)


def _peer_select(x2, gain, wqt, keys, tile):
    n, d = x2.shape
    per_head = lambda: pl.BlockSpec((PEER_HEADS, PEER_KEYS, tile), lambda i: (0, 0, i))
    sel_shape = (PEER_HEADS, PEER_KEYS, n)
    return pl.pallas_call(
        _peer_select_kernel,
        grid=(n // tile,),
        in_specs=[pl.BlockSpec((tile, d), lambda i: (i, 0)), _full(gain.shape), _full(wqt.shape),
                  _full(keys.shape)],
        out_specs=[pl.BlockSpec((d, tile), lambda i: (0, i)),
                   per_head(), per_head(), per_head(), per_head()],
        out_shape=[jax.ShapeDtypeStruct((d, n), BF16),
                   jax.ShapeDtypeStruct(sel_shape, BF16), jax.ShapeDtypeStruct(sel_shape, BF16),
                   jax.ShapeDtypeStruct(sel_shape, jnp.uint32),
                   jax.ShapeDtypeStruct(sel_shape, jnp.uint32)],
        scratch_shapes=[pltpu.VMEM((wqt.shape[0], tile), BF16)],
        compiler_params=_params("parallel"),
        name="peer_select",
    )(x2, gain, wqt, keys)


def _peer_flat_kernel(x_ref, ht_ref, rank2_ref, e2_ref, cnt_ref, e1_ref, u_ref, vt_ref, fgain_ref,
                      o_ref, acc_ref, act_ref, ga_ref, *, final_norm):
    j = pl.program_id(1)
    tokens = ht_ref.shape[1]
    sub = PEER_KEYS // BF16_ROWS

    @pl.when(j == 0)
    def _():
        acc_ref[...] = jnp.zeros_like(acc_ref)

    def pair_row(ref, hd, row):
        words = jnp.broadcast_to(ref[hd, row:row + 1, :], (BF16_ROWS // 2, tokens))
        return pltpu.bitcast(words, BF16)

    act_ref[...] = _dot(u_ref[...], ht_ref[...])
    for ii in range(u_ref.shape[0] // PEER_KEYS):
        gate = jnp.zeros((sub, BF16_ROWS, tokens), BF16)
        for hd in range(PEER_HEADS):
            w = e2_ref[hd] * pair_row(e1_ref, hd, ii)[None]
            hit = rank2_ref[hd] < pair_row(cnt_ref, hd, ii)[None]
            gate = gate + jnp.where(hit, w, jnp.zeros_like(w))
        rows = slice(ii * PEER_KEYS, (ii + 1) * PEER_KEYS)
        a = act_ref[rows, :]
        gl = a / (1.0 + jnp.exp2(a * (GELU_K1 + GELU_K3 * (a * a))))
        ga_ref[rows, :] = (gl.astype(BF16).reshape(sub, BF16_ROWS, tokens) * gate
                           ).reshape(PEER_KEYS, tokens)
    acc_ref[...] += _dot(vt_ref[...], ga_ref[...])

    @pl.when(j == pl.num_programs(1) - 1)
    def _():
        y = x_ref[...] + acc_ref[...].T
        if final_norm:
            y = _rmsnorm(y, fgain_ref[...])
        o_ref[...] = y


def _peer_flat(x2, ht, rank2, e2, cnt, e1, u16, vt16, fgain, tile, etile, final_norm):
    n, d = x2.shape
    n_exp = u16.shape[0]
    rows = etile // PEER_KEYS
    packed = lambda: pl.BlockSpec((PEER_HEADS, PEER_KEYS // BF16_ROWS, BF16_ROWS, tile),
                                  lambda i, j: (0, 0, 0, i))
    row_words = lambda: pl.BlockSpec((PEER_HEADS, rows, tile), lambda i, j: (0, j, i))
    split_rows = lambda a: a.reshape(PEER_HEADS, PEER_KEYS // BF16_ROWS, BF16_ROWS, n)
    return pl.pallas_call(
        functools.partial(_peer_flat_kernel, final_norm=final_norm),
        grid=(n // tile, n_exp // etile),
        in_specs=[pl.BlockSpec((tile, d), lambda i, j: (i, 0)),
                  pl.BlockSpec((d, tile), lambda i, j: (0, i)),
                  packed(), packed(), row_words(), row_words(),
                  pl.BlockSpec((etile, d), lambda i, j: (j, 0)),
                  pl.BlockSpec((d, etile), lambda i, j: (0, j)),
                  pl.BlockSpec(fgain.shape, lambda i, j: (0, 0))],
        out_specs=pl.BlockSpec((tile, d), lambda i, j: (i, 0)),
        out_shape=jax.ShapeDtypeStruct((n, d), F32),
        scratch_shapes=[pltpu.VMEM((d, tile), F32), pltpu.VMEM((etile, tile), F32),
                        pltpu.VMEM((etile, tile), BF16)],
        compiler_params=_params("parallel", "arbitrary"),
        name="peer_flat",
    )(x2, ht, split_rows(rank2), split_rows(e2), cnt, e1, u16, vt16, fgain)


def _tile(n, want):
    t = min(want, n)
    while n % t:
        t //= 2
    return t


def kernel(x, mem, norm_mix, w_in, sg_v_gain, sg_w_spatial, sg_b_spatial, gla_w_gate, gla_b_gate,
           gla_out_gain, w_out, norm_mem, mem_gain, w_cq, w_ckv, w_co, norm_ffn, peer_w_q,
           peer_sub_keys, peer_u, peer_v, final_gain):
    b, s, d = x.shape
    n = b * s
    depth = w_in.shape[0]
    row_tile = _tile(s, 512)
    sb_tile = _tile(s, 2048)
    sel_tile = _tile(n, 256)
    main_tile = _tile(n, 512)
    etile = 2048

    sb_end = 3 * SB_WIDTH
    sg_end = sb_end + 2 * SG_WIDTH
    gl_end = sg_end + 2 * GLA_KEY_WIDTH + 2 * GLA_WIDTH
    x2 = x.reshape(n, d)
    for l in range(depth):
        w16 = w_in[l].astype(BF16)
        wga = jnp.pad(w16[:, gl_end:], ((0, 0), (0, LANES - GLA_GATE_RANK)))
        wgate = jnp.pad(gla_w_gate[l].astype(BF16), ((0, LANES - GLA_GATE_RANK), (0, 0)))
        sb, sg, gl, la = _inproj(x2, norm_mix[l][None], w16[:, :sb_end], w16[:, sb_end:sg_end],
                                 w16[:, sg_end:gl_end], wga, wgate, gla_b_gate[l][None], row_tile)
        ysb = _sb_attn(sb.reshape(b, s, -1), sb_tile)
        bias = jnp.repeat(sg_b_spatial[l].T, SG_GROUP_DIM, axis=1)
        yrest = _sgu_gla(sg.reshape(b, s, -1), gl.reshape(b, s, -1), la.reshape(b, s, -1),
                         sg_v_gain[l][None], sg_w_spatial[l], bias, gla_out_gain[l][None])
        x2 = _outproj(x2, ysb.reshape(n, -1), yrest.reshape(n, -1), w_out[l].astype(BF16), row_tile)

        kv = _mem_kv(mem, mem_gain[l][None], w_ckv[l].astype(BF16))
        x2 = _cross(x2, norm_mem[l][None], w_cq[l].astype(BF16), kv, w_co[l].astype(BF16),
                    row_tile, s)

        keys = peer_sub_keys[l].astype(BF16)
        ht, rank2, e2, cnt, e1 = _peer_select(x2, norm_ffn[l][None], peer_w_q[l].T.astype(BF16),
                                              keys, sel_tile)
        vt = peer_v[l].T.astype(BF16)
        x2 = _peer_flat(x2, ht, rank2, e2, cnt, e1, peer_u[l].astype(BF16),
                        vt, final_gain[None], main_tile, etile,
                        final_norm=(l == depth - 1))
    return x2.reshape(b, s, d)
```

```python
import functools

import jax
import jax.numpy as jnp
import numpy as np
from jax import lax
from jax.experimental import pallas as pl
from jax.experimental.pallas import tpu as pltpu

EPS = 1e-6
LOG2E = 1.4426950408889634

LANES = 128
BF16_ROWS = 16
GELU_K1 = -2.0 * 0.7978845608028654 * LOG2E
GELU_K3 = GELU_K1 * 0.044715
VMEM_LIMIT_BYTES = 56 * 1024 * 1024

SB_HEADS = 8
SB_HEAD_DIM = 64
SB_WIDTH = SB_HEADS * SB_HEAD_DIM
SG_GROUPS = 4
SG_GROUP_DIM = 64
SG_WIDTH = SG_GROUPS * SG_GROUP_DIM
GLA_HEADS = 4
GLA_DK = 32
GLA_DV = 64
GLA_KEY_WIDTH = GLA_HEADS * GLA_DK
GLA_WIDTH = GLA_HEADS * GLA_DV
GLA_GATE_RANK = 16
GLA_TAU = 16.0
CHUNK = 128
X_HEADS = 4
PEER_HEADS = 8
PEER_KEYS = 128
PEER_TOPK = 16
PEER_HALF = 64
SB_UNROLL = 4

BF16 = jnp.bfloat16
F32 = jnp.float32


def _params(*semantics):
    return pltpu.CompilerParams(dimension_semantics=semantics,
                                vmem_limit_bytes=VMEM_LIMIT_BYTES)


def _dot(a, b):
    return jnp.dot(a, b, preferred_element_type=F32)


def _dot_nt(a, b):
    return lax.dot_general(a, b, (((1,), (1,)), ((), ())), preferred_element_type=F32)


def _split_dot(a, b16):
    hi = a.astype(BF16)
    lo = (a - hi.astype(F32)).astype(BF16)
    return _dot(hi, b16) + _dot(lo, b16)


def _rmsnorm(x, gain):
    return x * lax.rsqrt(jnp.mean(x * x, axis=-1, keepdims=True) + EPS) * gain


def _log_sigmoid(x):
    return jnp.minimum(x, 0.0) - jnp.log1p(jnp.exp(-jnp.abs(x)))


def _full(shape):
    n = len(shape)
    return pl.BlockSpec(shape, lambda *_: (0,) * n)


def _inproj_kernel(x_ref, gain_ref, wsb_ref, wsg_ref, wgl_ref, wga_ref, wgate_ref, bgate_ref,
                   sb_ref, sg_ref, gl_ref, la_ref):
    h = _rmsnorm(x_ref[...], gain_ref[...]).astype(BF16)
    sb = _dot(h, wsb_ref[...])
    sb_ref[:, :SB_WIDTH] = (sb[:, :SB_WIDTH] * (SB_HEAD_DIM ** -0.5 * LOG2E)).astype(BF16)
    sb_ref[:, SB_WIDTH:] = sb[:, SB_WIDTH:].astype(BF16)
    sg_ref[...] = _dot(h, wsg_ref[...])
    gl_ref[...] = _dot(h, wgl_ref[...])
    ga = _dot(h, wga_ref[...])
    gate = _dot(ga.astype(BF16), wgate_ref[...]) + bgate_ref[...]
    la_ref[...] = _log_sigmoid(gate) * (1.0 / GLA_TAU)


def _inproj(x2, gain, wsb, wsg, wgl, wga, wgate, bgate, tile):
    n, d = x2.shape
    row = lambda w: pl.BlockSpec((tile, w), lambda i: (i, 0))
    return pl.pallas_call(
        _inproj_kernel,
        grid=(n // tile,),
        in_specs=[row(d), _full(gain.shape), _full(wsb.shape), _full(wsg.shape), _full(wgl.shape),
                  _full(wga.shape), _full(wgate.shape), _full(bgate.shape)],
        out_specs=[row(3 * SB_WIDTH), row(2 * SG_WIDTH), row(2 * GLA_KEY_WIDTH + 2 * GLA_WIDTH),
                   row(GLA_KEY_WIDTH)],
        out_shape=[jax.ShapeDtypeStruct((n, 3 * SB_WIDTH), BF16),
                   jax.ShapeDtypeStruct((n, 2 * SG_WIDTH), F32),
                   jax.ShapeDtypeStruct((n, 2 * GLA_KEY_WIDTH + 2 * GLA_WIDTH), F32),
                   jax.ShapeDtypeStruct((n, GLA_KEY_WIDTH), F32)],
        compiler_params=_params("parallel"),
        name="inproj",
    )(x2, gain, wsb, wsg, wgl, wga, wgate, bgate)


def _sb_kernel(q_ref, k_ref, v_ref, o_ref, acc_ref, carry_ref, *, tq):
    qi = pl.program_id(2)
    nsub = tq // CHUNK
    lane = lax.broadcasted_iota(jnp.int32, (CHUNK, LANES), 1)
    head0 = lane < SB_HEAD_DIM
    r_i = lax.broadcasted_iota(jnp.int32, (2 * LANES, 2 * LANES), 0)
    c_i = lax.broadcasted_iota(jnp.int32, (2 * LANES, 2 * LANES), 1)
    m_incl = jnp.where((r_i >= c_i) & ((r_i >= LANES) == (c_i >= LANES)), -1.0, 0.0).astype(BF16)

    acc_ref[...] = jnp.zeros_like(acc_ref)
    carry_ref[...] = jnp.zeros_like(carry_ref)

    def blocks(kjs, row0, masked):
        rows = slice(row0, tq)
        nrow = tq - row0
        q = q_ref[0, rows, :]
        carry = carry_ref[rows, :]
        acc = acc_ref[rows, :]
        for kj in kjs:
            start = pl.multiple_of(kj * CHUNK, CHUNK)
            kb = k_ref[0, pl.ds(start, CHUNK), :]
            vb = v_ref[0, pl.ds(start, CHUNK), :]
            zero = jnp.zeros_like(kb)
            k2 = jnp.concatenate([jnp.where(head0, kb, zero), jnp.where(head0, zero, kb)], axis=0)
            v2 = jnp.concatenate([jnp.where(head0, vb, zero), jnp.where(head0, zero, vb)], axis=0)
            z = _dot_nt(q, k2)
            nl = jnp.maximum(z, 0.0) + jnp.log(1.0 + jnp.exp2(-jnp.abs(z))) * LOG2E
            if masked:
                t_loc = row0 + lax.broadcasted_iota(jnp.int32, (nrow, 2 * LANES), 0)
                s_loc = row0 + (lax.broadcasted_iota(jnp.int32, (nrow, 2 * LANES), 1) & (LANES - 1))
                mask = s_loc < t_loc
                nl = jnp.where(mask, nl, 0.0)
            incl = _dot(nl.astype(BF16), m_incl)
            w = jnp.exp2(z + incl + carry)
            if masked:
                w = jnp.where(mask, w, 0.0)
            acc = acc + _dot(w.astype(BF16), v2)
            carry = carry + jnp.concatenate(
                [jnp.broadcast_to(incl[:, 0:1], (nrow, LANES)),
                 jnp.broadcast_to(incl[:, LANES:LANES + 1], (nrow, LANES))], axis=1)
        carry_ref[rows, :] = carry
        acc_ref[rows, :] = acc

    for c in reversed(range(nsub)):
        blocks([qi * nsub + c], c * CHUNK, True)

    def body(it, _):
        first = qi * nsub - 1 - it * SB_UNROLL
        blocks([first - u for u in range(SB_UNROLL)], 0, False)
        return 0

    lax.fori_loop(0, qi * (nsub // SB_UNROLL), body, 0)
    o_ref[0] = acc_ref[...].astype(BF16)


def _sb_attn(sb, tq):
    b, s, _ = sb.shape
    pairs = SB_WIDTH // LANES
    return pl.pallas_call(
        functools.partial(_sb_kernel, tq=tq),
        grid=(b, pairs, s // tq),
        in_specs=[pl.BlockSpec((1, tq, LANES), lambda bi, hp, qi: (bi, qi, hp)),
                  pl.BlockSpec((1, s, LANES), lambda bi, hp, qi: (bi, 0, pairs + hp)),
                  pl.BlockSpec((1, s, LANES), lambda bi, hp, qi: (bi, 0, 2 * pairs + hp))],
        out_specs=pl.BlockSpec((1, tq, LANES), lambda bi, hp, qi: (bi, qi, hp)),
        out_shape=jax.ShapeDtypeStruct((b, s, SB_WIDTH), BF16),
        scratch_shapes=[pltpu.VMEM((tq, LANES), F32), pltpu.VMEM((tq, 2 * LANES), F32)],
        compiler_params=_params("parallel", "parallel", "parallel"),
        name="sb_attn",
    )(sb, sb, sb)


GLA_LEVELS = 7


def _gla_level_tables():
    r = np.arange(CHUNK)
    sel = np.zeros((GLA_LEVELS, CHUNK, CHUNK), np.float32)
    upper = np.zeros((GLA_LEVELS, CHUNK, GLA_KEY_WIDTH), np.float32)
    pair = np.zeros((GLA_LEVELS + 1, CHUNK, GLA_HEADS, CHUNK), np.float32)
    for lv in range(GLA_LEVELS):
        half = CHUNK >> (lv + 1)
        blk = r // half
        is_upper = blk % 2 == 1
        ref = np.where(is_upper, blk * half - 1, (blk + 1) * half - 1)
        sel[lv, r, ref] = 1.0
        upper[lv, is_upper, :] = 1.0
        pair[lv] = ((r[:, None] // (2 * half)) == (r[None, :] // (2 * half)))[:, None, :]
    pair[GLA_LEVELS] = (r[:, None] == r[None, :])[:, None, :]
    return (jnp.asarray(sel.reshape(GLA_LEVELS * CHUNK, CHUNK), BF16), jnp.asarray(upper),
            jnp.asarray(pair.reshape(GLA_LEVELS + 1, CHUNK, GLA_HEADS * CHUNK)))


def _sgu_gla_kernel(sg_ref, gl_ref, la_ref, sgain_ref, ws_ref, bs_ref, ogain_ref, sel_ref, upper_ref,
                    pair_ref, y_ref, state_ref):
    @pl.when(pl.program_id(1) == 0)
    def _():
        state_ref[...] = jnp.zeros_like(state_ref)

    row = lax.broadcasted_iota(jnp.int32, (CHUNK, CHUNK), 0)
    col = lax.broadcasted_iota(jnp.int32, (CHUNK, CHUNK), 1)
    causal = row >= col
    lane_w = lax.broadcasted_iota(jnp.int32, (CHUNK, SG_WIDTH), 1)

    u = jax.nn.gelu(sg_ref[0, :, :SG_WIDTH])
    v = _rmsnorm(jax.nn.gelu(sg_ref[0, :, SG_WIDTH:]), sgain_ref[...]).astype(BF16)
    mixed = bs_ref[...]
    for g in range(SG_GROUPS):
        wg = jnp.where(causal, ws_ref[g], 0.0).astype(BF16)
        in_group = (lane_w >= g * SG_GROUP_DIM) & (lane_w < (g + 1) * SG_GROUP_DIM)
        mixed = mixed + _dot(wg, jnp.where(in_group, v, jnp.zeros_like(v)))
    y_ref[0, :, :SG_WIDTH] = (u * mixed).astype(BF16)

    kw = GLA_KEY_WIDTH
    q = gl_ref[0, :, :kw] * GLA_DK ** -0.5
    k = gl_ref[0, :, kw:2 * kw]
    vv = gl_ref[0, :, 2 * kw:2 * kw + GLA_WIDTH]
    og = gl_ref[0, :, 2 * kw + GLA_WIDTH:]
    tri = jnp.where(causal, 1.0, 0.0).astype(BF16)
    la = la_ref[0]
    la_hi = la.astype(BF16)
    la_lo = (la - la_hi.astype(F32)).astype(BF16)
    b = _dot(tri, la_hi) + _dot(tri, la_lo)
    b_t = b.T
    k_t = k.T
    b_end_col = b_t[:, CHUNK - 1:CHUNK]

    lane_v = lax.broadcasted_iota(jnp.int32, (CHUNK, GLA_WIDTH), 1)
    lane_k = lax.broadcasted_iota(jnp.int32, (CHUNK, kw), 1)
    state = state_ref[...]
    o = _dot((q * jnp.exp(b)).astype(BF16), state.astype(BF16))
    vv16 = vv.astype(BF16)

    b_hi = b.astype(BF16)
    b_mid = (b - b_hi.astype(F32)).astype(BF16)
    b_lo = (b - b_hi.astype(F32) - b_mid.astype(F32)).astype(BF16)
    b_ref = _dot(sel_ref[...], b_hi) + _dot(sel_ref[...], b_mid) + _dot(sel_ref[...], b_lo)
    key_heads = [(lane_k >= h * GLA_DK) & (lane_k < (h + 1) * GLA_DK) for h in range(GLA_HEADS)]
    scores = jnp.zeros((CHUNK, GLA_HEADS * CHUNK), F32)
    for lv in range(GLA_LEVELS + 1):
        if lv < GLA_LEVELS:
            f = jnp.exp(-jnp.abs(b - b_ref[lv * CHUNK:(lv + 1) * CHUNK, :]))
            ql = (q * f * upper_ref[lv]).astype(BF16)
            kl = (k * f * (1.0 - upper_ref[lv])).astype(BF16)
        else:
            ql, kl = q.astype(BF16), k.astype(BF16)
        kl4 = jnp.concatenate([jnp.where(m, kl, jnp.zeros_like(kl)) for m in key_heads], axis=0)
        scores = scores + _dot_nt(ql, kl4) * pair_ref[lv]
    v4 = jnp.concatenate(
        [jnp.where((lane_v >= h * GLA_DV) & (lane_v < (h + 1) * GLA_DV), vv16, jnp.zeros_like(vv16))
         for h in range(GLA_HEADS)], axis=0)
    o = o + _dot(scores.astype(BF16), v4)

    kd_t = (k_t * jnp.exp(b_end_col - b_t)).astype(BF16)
    row_h = lax.broadcasted_iota(jnp.int32, (kw, GLA_WIDTH), 0) // GLA_DK
    col_h = lax.broadcasted_iota(jnp.int32, (kw, GLA_WIDTH), 1) // GLA_DV
    state_ref[...] = jnp.exp(b_end_col) * state + jnp.where(row_h == col_h, _dot(kd_t, vv16), 0.0)

    r2 = lax.broadcasted_iota(jnp.int32, (GLA_WIDTH, GLA_WIDTH), 0) // GLA_DV
    c2 = lax.broadcasted_iota(jnp.int32, (GLA_WIDTH, GLA_WIDTH), 1) // GLA_DV
    avg = jnp.where(r2 == c2, 1.0 / GLA_DV, 0.0).astype(BF16)
    msq = _split_dot(o * o, avg)
    y = o * lax.rsqrt(msq + EPS) * ogain_ref[...]
    y_ref[0, :, SG_WIDTH:] = (y * (og * (1.0 / (1.0 + jnp.exp(-og))))).astype(BF16)


def _sgu_gla(sg, gl, la, sgain, ws, bs, ogain):
    b, s, _ = sg.shape
    blk = lambda w: pl.BlockSpec((1, CHUNK, w), lambda bi, ci: (bi, ci, 0))
    sel, upper, pair = _gla_level_tables()
    return pl.pallas_call(
        _sgu_gla_kernel,
        grid=(b, s // CHUNK),
        in_specs=[blk(sg.shape[2]), blk(gl.shape[2]), blk(la.shape[2]),
                  _full(sgain.shape), _full(ws.shape), _full(bs.shape), _full(ogain.shape),
                  _full(sel.shape), _full(upper.shape), _full(pair.shape)],
        out_specs=blk(SG_WIDTH + GLA_WIDTH),
        out_shape=jax.ShapeDtypeStruct((b, s, SG_WIDTH + GLA_WIDTH), BF16),
        scratch_shapes=[pltpu.VMEM((GLA_KEY_WIDTH, GLA_WIDTH), F32)],
        compiler_params=_params("parallel", "arbitrary"),
        name="sgu_gla",
    )(sg, gl, la, sgain, ws, bs, ogain, sel, upper, pair)


def _outproj_kernel(x_ref, ysb_ref, yrest_ref, w_ref, o_ref):
    o_ref[...] = (x_ref[...] + _dot(ysb_ref[...], w_ref[:SB_WIDTH, :])
                  + _dot(yrest_ref[...], w_ref[SB_WIDTH:, :]))


def _outproj(x2, ysb, yrest, w, tile):
    n, d = x2.shape
    row = lambda w_: pl.BlockSpec((tile, w_), lambda i: (i, 0))
    return pl.pallas_call(
        _outproj_kernel,
        grid=(n // tile,),
        in_specs=[row(d), row(ysb.shape[1]), row(yrest.shape[1]), _full(w.shape)],
        out_specs=row(d),
        out_shape=jax.ShapeDtypeStruct((n, d), F32),
        compiler_params=_params("parallel"),
        name="outproj",
    )(x2, ysb, yrest, w)


def _mem_kv_kernel(mem_ref, gain_ref, w_ref, kv_ref):
    m = _rmsnorm(mem_ref[0], gain_ref[...]).astype(BF16)
    kv_ref[0] = _dot(m, w_ref[...]).astype(BF16)


def _mem_kv(mem, gain, w):
    b, m, d = mem.shape
    return pl.pallas_call(
        _mem_kv_kernel,
        grid=(b,),
        in_specs=[pl.BlockSpec((1, m, d), lambda bi: (bi, 0, 0)), _full(gain.shape), _full(w.shape)],
        out_specs=pl.BlockSpec((1, m, 2 * d), lambda bi: (bi, 0, 0)),
        out_shape=jax.ShapeDtypeStruct((b, m, 2 * d), BF16),
        compiler_params=_params("parallel"),
        name="mem_kv",
    )(mem, gain, w)


def _cross_kernel(x_ref, gain_ref, wq_ref, kv_ref, wo_ref, o_ref, att_ref):
    x = x_ref[...]
    d = x.shape[1]
    hd = d // X_HEADS
    h = _rmsnorm(x, gain_ref[...]).astype(BF16)
    q = (_dot(h, wq_ref[...]) * hd ** -0.5).astype(BF16)
    for a in range(X_HEADS):
        kh = kv_ref[0, :, a * hd:(a + 1) * hd]
        vh = kv_ref[0, :, d + a * hd:d + (a + 1) * hd]
        s = _dot_nt(q[:, a * hd:(a + 1) * hd], kh)
        e = jnp.exp(s - jnp.max(s, axis=-1, keepdims=True))
        p = e / jnp.sum(e, axis=-1, keepdims=True)
        att_ref[:, a * hd:(a + 1) * hd] = _dot(p.astype(BF16), vh).astype(BF16)
    o_ref[...] = x + _dot(att_ref[...], wo_ref[...])


def _cross(x2, gain, wq, kv, wo, tile, seq):
    n, d = x2.shape
    tiles_per_seq = seq // tile
    row = pl.BlockSpec((tile, d), lambda i: (i, 0))
    return pl.pallas_call(
        _cross_kernel,
        grid=(n // tile,),
        in_specs=[row, _full(gain.shape), _full(wq.shape),
                  pl.BlockSpec((1,) + kv.shape[1:], lambda i: (i // tiles_per_seq, 0, 0)),
                  _full(wo.shape)],
        out_specs=row,
        out_shape=jax.ShapeDtypeStruct((n, d), F32),
        scratch_shapes=[pltpu.VMEM((tile, d), BF16)],
        compiler_params=_params("parallel"),
        name="cross",
    )(x2, gain, wq, kv, wo)


_CAND_PAIRS = [(r1, r2) for r1 in range(PEER_TOPK) for r2 in range(PEER_TOPK)
               if (r1 + 1) * (r2 + 1) <= PEER_TOPK]


def _top16(s):
    work = s
    rank = jnp.full(s.shape, float(PEER_KEYS - 1), F32)
    vals = []
    for r in range(PEER_TOPK):
        m = jnp.max(work, axis=0, keepdims=True)
        hit = work == m
        rank = jnp.where(hit, float(r), rank)
        work = jnp.where(hit, -jnp.inf, work)
        vals.append(m)
    return vals, rank


def _prefix_count(s1, v2, tau):
    def pick(bits, lo, step):
        if not bits:
            return v2[lo + step - 1]
        taken = pick(bits[1:], lo + bits[0][1], step)
        skipped = pick(bits[1:], lo, step)
        return jnp.where(bits[0][0], taken, skipped)

    bits = []
    cnt = None
    for step in (8, 4, 2, 1):
        passed = s1 + pick(bits, 0, step) >= tau
        bits.append((passed, step))
        add = jnp.where(passed, float(step), 0.0)
        cnt = add if cnt is None else cnt + add
    return jnp.where(s1 + v2[PEER_TOPK - 1] >= tau, float(PEER_TOPK), cnt)


def _bf16_pair_words(x):
    u = lax.bitcast_convert_type(x.astype(BF16).astype(F32), jnp.uint32)
    return u | (u >> 16)


def _peer_select_kernel(x_ref, gain_ref, wqt_ref, keys_ref, ht_ref, rank2_ref, e2_ref, cnt_ref, e1_ref,
                        qt_ref):
    h_t = _rmsnorm(x_ref[...], gain_ref[...]).T.astype(BF16)
    ht_ref[...] = h_t
    qt_ref[...] = _dot(wqt_ref[...], h_t).astype(BF16)
    tokens = h_t.shape[1]

    def head(hd):
        base = hd * 2 * PEER_HALF
        s1 = _dot(keys_ref[hd, 0], qt_ref[pl.ds(base, PEER_HALF), :])
        s2 = _dot(keys_ref[hd, 1], qt_ref[pl.ds(base + PEER_HALF, PEER_HALF), :])
        v1, _ = _top16(s1)
        v2, rank2 = _top16(s2)
        cands = [v1[r1] + v2[r2] for r1, r2 in _CAND_PAIRS]
        pad = -len(cands) % 8
        cands = jnp.concatenate(cands + [jnp.full((pad, tokens), -jnp.inf, F32)], axis=0)
        top = []
        for _ in range(PEER_TOPK):
            m = jnp.max(cands, axis=0, keepdims=True)
            cands = jnp.where(cands == m, -jnp.inf, cands)
            top.append(m)
        tau = top[-1]
        top = jnp.concatenate(top, axis=0)
        z = jnp.sum(jnp.exp(top - top[0:1, :]), axis=0, keepdims=True)
        cnt = _prefix_count(s1, v2, tau)
        rank2_ref[hd] = rank2.astype(BF16)
        e2_ref[hd] = jnp.exp(s2 - v2[0]).astype(BF16)
        cnt_ref[hd] = _bf16_pair_words(cnt)
        e1_ref[hd] = _bf16_pair_words(jnp.exp(s1 - v1[0]) / z)

    for hd in range(PEER_HEADS):
        head(hd)


def _peer_select(x2, gain, wqt, keys, tile):
    n, d = x2.shape
    per_head = lambda: pl.BlockSpec((PEER_HEADS, PEER_KEYS, tile), lambda i: (0, 0, i))
    sel_shape = (PEER_HEADS, PEER_KEYS, n)
    return pl.pallas_call(
        _peer_select_kernel,
        grid=(n // tile,),
        in_specs=[pl.BlockSpec((tile, d), lambda i: (i, 0)), _full(gain.shape), _full(wqt.shape),
                  _full(keys.shape)],
        out_specs=[pl.BlockSpec((d, tile), lambda i: (0, i)),
                   per_head(), per_head(), per_head(), per_head()],
        out_shape=[jax.ShapeDtypeStruct((d, n), BF16),
                   jax.ShapeDtypeStruct(sel_shape, BF16), jax.ShapeDtypeStruct(sel_shape, BF16),
                   jax.ShapeDtypeStruct(sel_shape, jnp.uint32),
                   jax.ShapeDtypeStruct(sel_shape, jnp.uint32)],
        scratch_shapes=[pltpu.VMEM((wqt.shape[0], tile), BF16)],
        compiler_params=_params("parallel"),
        name="peer_select",
    )(x2, gain, wqt, keys)


def _peer_flat_kernel(x_ref, ht_ref, rank2_ref, e2_ref, cnt_ref, e1_ref, u_ref, vt_ref, fgain_ref,
                      o_ref, acc_ref, act_ref, ga_ref, *, final_norm):
    j = pl.program_id(1)
    tokens = ht_ref.shape[1]
    sub = PEER_KEYS // BF16_ROWS

    @pl.when(j == 0)
    def _():
        acc_ref[...] = jnp.zeros_like(acc_ref)

    def pair_row(ref, hd, row):
        words = jnp.broadcast_to(ref[hd, row:row + 1, :], (BF16_ROWS // 2, tokens))
        return pltpu.bitcast(words, BF16)

    act_ref[...] = _dot(u_ref[...], ht_ref[...])
    for ii in range(u_ref.shape[0] // PEER_KEYS):
        gate = None
        for hd in range(PEER_HEADS):
            w = e2_ref[hd] * pair_row(e1_ref, hd, ii)[None]
            hit = rank2_ref[hd] < pair_row(cnt_ref, hd, ii)[None]
            part = jnp.where(hit, w, jnp.zeros_like(w))
            gate = part if gate is None else gate + part
        rows = slice(ii * PEER_KEYS, (ii + 1) * PEER_KEYS)
        a = act_ref[rows, :]
        gl = a / (1.0 + jnp.exp2(a * (GELU_K1 + GELU_K3 * (a * a))))
        ga_ref[rows, :] = (gl.astype(BF16).reshape(sub, BF16_ROWS, tokens) * gate
                           ).reshape(PEER_KEYS, tokens)
    acc_ref[...] += _dot(vt_ref[...], ga_ref[...])

    @pl.when(j == pl.num_programs(1) - 1)
    def _():
        y = x_ref[...] + acc_ref[...].T
        if final_norm:
            y = _rmsnorm(y, fgain_ref[...])
        o_ref[...] = y


def _peer_flat(x2, ht, rank2, e2, cnt, e1, u16, vt16, fgain, tile, etile, final_norm):
    n, d = x2.shape
    n_exp = u16.shape[0]
    rows = etile // PEER_KEYS
    packed = lambda: pl.BlockSpec((PEER_HEADS, PEER_KEYS // BF16_ROWS, BF16_ROWS, tile),
                                  lambda i, j: (0, 0, 0, i))
    row_words = lambda: pl.BlockSpec((PEER_HEADS, rows, tile), lambda i, j: (0, j, i))
    split_rows = lambda a: a.reshape(PEER_HEADS, PEER_KEYS // BF16_ROWS, BF16_ROWS, n)
    return pl.pallas_call(
        functools.partial(_peer_flat_kernel, final_norm=final_norm),
        grid=(n // tile, n_exp // etile),
        in_specs=[pl.BlockSpec((tile, d), lambda i, j: (i, 0)),
                  pl.BlockSpec((d, tile), lambda i, j: (0, i)),
                  packed(), packed(), row_words(), row_words(),
                  pl.BlockSpec((etile, d), lambda i, j: (j, 0)),
                  pl.BlockSpec((d, etile), lambda i, j: (0, j)),
                  pl.BlockSpec(fgain.shape, lambda i, j: (0, 0))],
        out_specs=pl.BlockSpec((tile, d), lambda i, j: (i, 0)),
        out_shape=jax.ShapeDtypeStruct((n, d), F32),
        scratch_shapes=[pltpu.VMEM((d, tile), F32), pltpu.VMEM((etile, tile), F32),
                        pltpu.VMEM((etile, tile), BF16)],
        compiler_params=_params("parallel", "arbitrary"),
        name="peer_flat",
    )(x2, ht, split_rows(rank2), split_rows(e2), cnt, e1, u16, vt16, fgain)


def _tile(n, want):
    t = min(want, n)
    while n % t:
        t //= 2
    return t


def kernel(x, mem, norm_mix, w_in, sg_v_gain, sg_w_spatial, sg_b_spatial, gla_w_gate, gla_b_gate,
           gla_out_gain, w_out, norm_mem, mem_gain, w_cq, w_ckv, w_co, norm_ffn, peer_w_q,
           peer_sub_keys, peer_u, peer_v, final_gain):
    b, s, d = x.shape
    n = b * s
    depth = w_in.shape[0]
    row_tile = _tile(s, 512)
    sb_tile = _tile(s, 2048)
    sel_tile = _tile(n, 256)
    main_tile = _tile(n, 512)
    etile = 2048

    sb_end = 3 * SB_WIDTH
    sg_end = sb_end + 2 * SG_WIDTH
    gl_end = sg_end + 2 * GLA_KEY_WIDTH + 2 * GLA_WIDTH
    x2 = x.reshape(n, d)
    for l in range(depth):
        w16 = w_in[l].astype(BF16)
        wga = jnp.pad(w16[:, gl_end:], ((0, 0), (0, LANES - GLA_GATE_RANK)))
        wgate = jnp.pad(gla_w_gate[l].astype(BF16), ((0, LANES - GLA_GATE_RANK), (0, 0)))
        sb, sg, gl, la = _inproj(x2, norm_mix[l][None], w16[:, :sb_end], w16[:, sb_end:sg_end],
                                 w16[:, sg_end:gl_end], wga, wgate, gla_b_gate[l][None], row_tile)
        ysb = _sb_attn(sb.reshape(b, s, -1), sb_tile)
        bias = jnp.repeat(sg_b_spatial[l].T, SG_GROUP_DIM, axis=1)
        yrest = _sgu_gla(sg.reshape(b, s, -1), gl.reshape(b, s, -1), la.reshape(b, s, -1),
                         sg_v_gain[l][None], sg_w_spatial[l], bias, gla_out_gain[l][None])
        x2 = _outproj(x2, ysb.reshape(n, -1), yrest.reshape(n, -1), w_out[l].astype(BF16), row_tile)

        kv = _mem_kv(mem, mem_gain[l][None], w_ckv[l].astype(BF16))
        x2 = _cross(x2, norm_mem[l][None], w_cq[l].astype(BF16), kv, w_co[l].astype(BF16),
                    row_tile, s)

        keys = peer_sub_keys[l].astype(BF16)
        ht, rank2, e2, cnt, e1 = _peer_select(x2, norm_ffn[l][None], peer_w_q[l].T.astype(BF16),
                                              keys, sel_tile)
        vt = peer_v[l].T.astype(BF16)
        x2 = _peer_flat(x2, ht, rank2, e2, cnt, e1, peer_u[l].astype(BF16),
                        vt, final_gain[None], main_tile, etile,
                        final_norm=(l == depth - 1))
    return x2.reshape(b, s, d)
```
